```python
import math
import jax, jax.numpy as jnp
from jax import lax
import numpy as np

D_MODEL = 1024
BATCH = 2
SEQ = 16384
DEPTH = 2

N_MIXERS = 2
HEAD_DIM = 64
FOX_HEADS = D_MODEL // HEAD_DIM
DIL_GROUPS = ((128, 1), (512, 4), (2048, 16))
N_GROUPS = len(DIL_GROUPS)
DIL_HEADS = D_MODEL // HEAD_DIM
ROT_DIM = HEAD_DIM // 4
ROPE_THETA = 500000.0
D_FF = 4 * D_MODEL
Q_BLOCK = 128
EPS = 1e-6
NEG_INF = -1e30
N_FOX_LAYERS = (DEPTH + 1) // 2
N_DIL_LAYERS = DEPTH // 2

kernel_name = "fox_dilated_hybrid_trunk"


def _rmsnorm(x, g):
    xf = x.astype(jnp.float32)
    y = xf * lax.rsqrt(jnp.mean(xf * xf, axis=-1, keepdims=True) + EPS)
    return (y * g.astype(jnp.float32)).astype(x.dtype)


def _partial_rope(x, positions):
    half = ROT_DIM // 2
    inv = ROPE_THETA ** (-jnp.arange(half, dtype=jnp.float32) * 2.0 / ROT_DIM)
    ang = positions.astype(jnp.float32)[:, :, None] * inv
    cos = jnp.cos(ang)[:, :, None, :]
    sin = jnp.sin(ang)[:, :, None, :]
    xf = x.astype(jnp.float32)
    x1 = xf[..., :half]
    x2 = xf[..., half:ROT_DIM]
    out = jnp.concatenate([x1 * cos - x2 * sin, x2 * cos + x1 * sin, xf[..., ROT_DIM:]], axis=-1)
    return out.astype(x.dtype)


def _fox_attention(h, w_in, b_f, q_gain, k_gain, w_out):
    B, S, _ = h.shape
    H, Dh = FOX_HEADS, HEAD_DIM
    proj = h @ w_in
    q = _rmsnorm(proj[..., :H * Dh].reshape(B, S, H, Dh), q_gain)
    k = _rmsnorm(proj[..., H * Dh:2 * H * Dh].reshape(B, S, H, Dh), k_gain)
    v = proj[..., 2 * H * Dh:3 * H * Dh].reshape(B, S, H, Dh)
    log_f = jax.nn.log_sigmoid((proj[..., 3 * H * Dh:] + b_f).astype(jnp.float32))
    c = jnp.cumsum(log_f, axis=1).transpose(0, 2, 1)
    scale = 1.0 / math.sqrt(Dh)
    outs = []
    for i in range(S // Q_BLOCK):
        lo, hi = i * Q_BLOCK, (i + 1) * Q_BLOCK
        s = jnp.einsum('bqhd,bkhd->bhqk', q[:, lo:hi], k[:, :hi]).astype(jnp.float32) * scale
        s = s + (c[:, :, lo:hi, None] - c[:, :, None, :hi])
        causal = jnp.arange(hi)[None, :] <= jnp.arange(lo, hi)[:, None]
        s = jnp.where(causal, s, NEG_INF)
        p = jax.nn.softmax(s, axis=-1)
        outs.append(jnp.einsum('bhqk,bkhd->bqhd', p.astype(v.dtype), v[:, :hi]))
    o = jnp.concatenate(outs, axis=1).reshape(B, S, H * Dh)
    return o @ w_out


def _dilated_group(q, k, v, window, dilation):
    B, S, H, Dh = q.shape
    d = dilation
    L = S // d
    n = window // d
    blk = n
    nb = -(-L // blk)
    Lp = nb * blk

    def to_sub(t):
        t = t.reshape(B, L, d, H, Dh).transpose(0, 2, 1, 3, 4)
        t = jnp.pad(t, ((0, 0), (0, 0), (0, Lp - L), (0, 0), (0, 0)))
        return t.reshape(B, d, nb, blk, H, Dh)

    def with_prev(t):
        prev = jnp.pad(t[:, :, :-1], ((0, 0), (0, 0), (1, 0), (0, 0), (0, 0), (0, 0)))
        return jnp.concatenate([prev, t], axis=3)

    qs = to_sub(q)
    kk = with_prev(to_sub(k))
    vv = with_prev(to_sub(v))
    scale = 1.0 / math.sqrt(Dh)
    s = jnp.einsum('brnqhe,brnkhe->brnhqk', qs, kk).astype(jnp.float32) * scale
    qi = jnp.arange(blk)[:, None]
    kj = jnp.arange(2 * blk)[None, :]
    dist = qi + blk - kj
    band = (dist >= 0) & (dist <= n)
    has_prev = (jnp.arange(nb)[:, None, None] > 0) | (kj[None] >= blk)
    mask = band[None] & has_prev
    s = jnp.where(mask[None, None, :, None], s, NEG_INF)
    m = jnp.max(s, axis=-1, keepdims=True)
    p = jnp.exp(s - m)
    den = jnp.sum(p, axis=-1, keepdims=True)
    o = jnp.einsum('brnhqk,brnkhe->brnqhe', (p / den).astype(vv.dtype), vv)
    lse = (m + jnp.log(den))[..., 0]
    o = o.reshape(B, d, Lp, H, Dh)[:, :, :L].transpose(0, 2, 1, 3, 4).reshape(B, S, H, Dh)
    lse = lse.transpose(0, 1, 2, 4, 3).reshape(B, d, Lp, H)[:, :, :L]
    lse = lse.transpose(0, 2, 1, 3).reshape(B, S, H)
    return o, lse


def _dilated_attention(h, positions, w_in, q_gain, k_gain, w_out):
    B, S, _ = h.shape
    H, Dh = DIL_HEADS, HEAD_DIM
    proj = (h @ w_in).reshape(B, S, N_GROUPS, 3, H, Dh)
    outs, lses = [], []
    for g, (window, dil) in enumerate(DIL_GROUPS):
        q = _partial_rope(_rmsnorm(proj[:, :, g, 0], q_gain[g]), positions)
        k = _partial_rope(_rmsnorm(proj[:, :, g, 1], k_gain[g]), positions)
        o, lse = _dilated_group(q, k, proj[:, :, g, 2], window, dil)
        outs.append(o)
        lses.append(lse)
    wts = jax.nn.softmax(jnp.stack(lses, axis=0), axis=0)
    o = jnp.einsum('gbsh,gbshe->bshe', wts, jnp.stack(outs, axis=0).astype(jnp.float32))
    return o.astype(h.dtype).reshape(B, S, H * Dh) @ w_out


def _sqrelu_mlp(h, w1, w2):
    return jnp.square(jax.nn.relu(h @ w1)) @ w2


def setup_inputs(seed: int = 0) -> dict:
    key = jax.random.key(seed)
    ks = jax.random.split(key, 16)
    D, Dh = D_MODEL, HEAD_DIM
    nA, nB = N_FOX_LAYERS, N_DIL_LAYERS
    nrm = lambda k, shape, fan: jax.random.normal(k, shape, jnp.float32) * fan ** -0.5
    gain = lambda k, shape: 1.0 + 0.02 * jax.random.normal(k, shape, jnp.float32)
    x = jax.random.normal(ks[0], (BATCH, SEQ, D), jnp.float32)
    offset = jax.random.randint(ks[1], (BATCH, 1), 0, 4096, dtype=jnp.int32)
    positions = offset + jnp.arange(SEQ, dtype=jnp.int32)[None, :]
    return {
        "x": x,
        "positions": positions,
        "fox_norm": gain(ks[2], (nA, D)),
        "fox_w_in": nrm(ks[3], (nA, D, 3 * FOX_HEADS * Dh + FOX_HEADS), D),
        "fox_b_f": jax.random.uniform(ks[4], (nA, FOX_HEADS), jnp.float32, 1.0, 5.0),
        "fox_q_gain": gain(ks[5], (nA, Dh)),
        "fox_k_gain": gain(ks[6], (nA, Dh)),
        "fox_w_out": nrm(ks[7], (nA, FOX_HEADS * Dh, D), FOX_HEADS * Dh),
        "dil_norm": gain(ks[8], (nB, D)),
        "dil_w_in": nrm(ks[9], (nB, D, N_GROUPS * 3 * DIL_HEADS * Dh), D),
        "dil_q_gain": gain(ks[10], (nB, N_GROUPS, Dh)),
        "dil_k_gain": gain(ks[11], (nB, N_GROUPS, Dh)),
        "dil_w_out": nrm(ks[12], (nB, DIL_HEADS * Dh, D), DIL_HEADS * Dh),
        "mlp_norm": gain(ks[13], (DEPTH, D)),
        "mlp_w1": nrm(ks[14], (DEPTH, D, D_FF), D),
        "mlp_w2": nrm(ks[15], (DEPTH, D_FF, D), D_FF),
    }


def reference(x, positions, fox_norm, fox_w_in, fox_b_f, fox_q_gain, fox_k_gain, fox_w_out,
              dil_norm, dil_w_in, dil_q_gain, dil_k_gain, dil_w_out,
              mlp_norm, mlp_w1, mlp_w2):
    for i in range(DEPTH):
        j = i // N_MIXERS
        if i % N_MIXERS == 0:
            h = _rmsnorm(x, fox_norm[j])
            x = x + _fox_attention(h, fox_w_in[j], fox_b_f[j], fox_q_gain[j], fox_k_gain[j], fox_w_out[j])
        else:
            h = _rmsnorm(x, dil_norm[j])
            x = x + _dilated_attention(h, positions, dil_w_in[j], dil_q_gain[j], dil_k_gain[j], dil_w_out[j])
        x = x + _sqrelu_mlp(_rmsnorm(x, mlp_norm[i]), mlp_w1[i], mlp_w2[i])
    return x
```

```python
import functools
import math

import jax
import jax.numpy as jnp
import numpy as np
from jax import lax
from jax.experimental import pallas as pl
from jax.experimental.pallas import tpu as pltpu

F32, BF16 = jnp.float32, jnp.bfloat16
HEAD_DIM = 64
ROT_DIM = 16
ROPE_THETA = 500000.0
DIL_GROUPS = ((128, 1), (512, 4), (2048, 16))
DIL_BLOCK = 128
EPS = 1e-6
NEG_INF = -1e30
LOG2E = 1.4426950408889634
BIAS_ROWS = 3
VMEM_LIMIT = 56 * 1024 * 1024

_NT = (((1,), (1,)), ((), ()))
_TN = (((0,), (0,)), ((), ()))


def _params(sem):
    return pltpu.CompilerParams(dimension_semantics=sem, vmem_limit_bytes=VMEM_LIMIT)


def _rms_bf16(x, g):
    return (x * lax.rsqrt(jnp.mean(x * x, axis=-1, keepdims=True) + EPS) * g).astype(BF16)


def _head_norm(y, gain):
    inv = lax.rsqrt(jnp.mean(y * y, axis=1, keepdims=True) + EPS)
    return y * inv * gain[None]


def _fox_inproj_kernel(x_ref, g_ref, wt_ref, wf_ref, bf_ref, gq_ref, gk_ref,
                       qkv_ref, c_ref, carry_ref, *, tiles_per_batch):
    i = pl.program_id(0)

    @pl.when(i % tiles_per_batch == 0)
    def _():
        carry_ref[...] = jnp.zeros_like(carry_ref)

    h = _rms_bf16(x_ref[...], g_ref[...])
    ts, d_model = x_ref.shape
    n_heads = d_model // HEAD_DIM
    for part, gain_ref, mult in ((0, gq_ref, LOG2E / math.sqrt(HEAD_DIM)), (1, gk_ref, 1.0)):
        rows = slice(part * d_model, (part + 1) * d_model)
        y = lax.dot_general(wt_ref[rows, :], h, _NT, preferred_element_type=F32)
        y = _head_norm(y.reshape(n_heads, HEAD_DIM, ts), gain_ref[...] * mult)
        qkv_ref[rows, :] = y.reshape(d_model, ts).astype(BF16)
    rows = slice(2 * d_model, 3 * d_model)
    qkv_ref[rows, :] = lax.dot_general(wt_ref[rows, :], h, _NT, preferred_element_type=F32).astype(BF16)

    f = lax.dot_general(wf_ref[...], h, _NT, preferred_element_type=F32) + bf_ref[...]
    lf = (jnp.minimum(f, 0.0) - jnp.log1p(jnp.exp(-jnp.abs(f)))) * LOG2E
    lane = lax.broadcasted_iota(jnp.int32, lf.shape, 1)
    shift = 1
    while shift < ts:
        lf = lf + jnp.where(lane >= shift, pltpu.roll(lf, shift, 1), 0.0)
        shift *= 2
    c = lf + carry_ref[:, 0:1]
    c_ref[...] = c
    carry_ref[...] = jnp.broadcast_to(c[:, ts - 1:ts], carry_ref.shape)


def _fox_inproj(x2, g, wt, wf, bf, gq, gk, *, seq, ts):
    tokens, d_model = x2.shape
    n_heads = d_model // HEAD_DIM
    const = lambda i: (0, 0)
    return pl.pallas_call(
        functools.partial(_fox_inproj_kernel, tiles_per_batch=seq // ts),
        grid=(tokens // ts,),
        in_specs=[
            pl.BlockSpec((ts, d_model), lambda i: (i, 0)),
            pl.BlockSpec((1, d_model), const),
            pl.BlockSpec((3 * d_model, d_model), const),
            pl.BlockSpec((n_heads, d_model), const),
            pl.BlockSpec((n_heads, 1), const),
            pl.BlockSpec((HEAD_DIM, 1), const),
            pl.BlockSpec((HEAD_DIM, 1), const),
        ],
        out_specs=[
            pl.BlockSpec((3 * d_model, ts), lambda i: (0, i)),
            pl.BlockSpec((n_heads, ts), lambda i: (0, i)),
        ],
        out_shape=[
            jax.ShapeDtypeStruct((3 * d_model, tokens), BF16),
            jax.ShapeDtypeStruct((n_heads, tokens), F32),
        ],
        scratch_shapes=[pltpu.VMEM((n_heads, 128), F32)],
        compiler_params=_params(("arbitrary",)),
        name="fox_inproj",
    )(x2, g, wt, wf, bf, gq, gk)


def _fox_attn_kernel(q_ref, k_ref, v_ref, c_ref, o_ref, kp_ref, *, tq):
    i = pl.program_id(2)
    seq = k_ref.shape[1]

    @pl.when(i == 0)
    def _build_keys():
        for n in range(seq // tq):
            sl = slice(n * tq, (n + 1) * tq)
            negc = -c_ref[:, sl]
            hi = negc.astype(BF16).astype(F32)
            rem = negc - hi
            mid = rem.astype(BF16).astype(F32)
            lo = rem - mid
            pad = jnp.zeros((HEAD_DIM - BIAS_ROWS, tq), F32)
            blk = jnp.concatenate([k_ref[:, sl].astype(F32), hi, mid, lo, pad], axis=0)
            kp_ref[sl, :] = blk.T.astype(BF16)

    ones_rows = (lax.broadcasted_iota(jnp.int32, (HEAD_DIM, tq), 0) < BIAS_ROWS).astype(BF16)
    qp = jnp.concatenate([q_ref[...], ones_rows], axis=0)

    def step(j, carry, masked):
        m, l, acc = carry
        off = pl.multiple_of(j * tq, tq)
        s = jnp.dot(kp_ref[pl.ds(off, tq), :], qp, preferred_element_type=F32)
        if masked:
            key = lax.broadcasted_iota(jnp.int32, s.shape, 0)
            qry = lax.broadcasted_iota(jnp.int32, s.shape, 1)
            s = jnp.where(key <= qry, s, NEG_INF)
        m_new = jnp.maximum(m, jnp.max(s, axis=0, keepdims=True))
        alpha = jnp.exp2(m - m_new)
        p = jnp.exp2(s - m_new)
        l = alpha * l + jnp.sum(p, axis=0, keepdims=True)
        pv = jnp.dot(v_ref[:, pl.ds(off, tq)], p.astype(BF16), preferred_element_type=F32)
        return m_new, l, alpha * acc + pv

    init = (jnp.full((1, tq), NEG_INF, F32), jnp.zeros((1, tq), F32), jnp.zeros((HEAD_DIM, tq), F32))
    carry = lax.fori_loop(0, i, lambda j, c: step(j, c, False), init)
    m, l, acc = step(i, carry, True)
    o_ref[...] = (acc / l).astype(BF16)


def _fox_attn(qkv, c3, *, batch, seq, tq):
    d_model = qkv.shape[0] // 3
    n_heads = d_model // HEAD_DIM
    tokens = qkv.shape[1]
    nq = seq // tq
    return pl.pallas_call(
        functools.partial(_fox_attn_kernel, tq=tq),
        grid=(batch, n_heads, nq),
        in_specs=[
            pl.BlockSpec((HEAD_DIM, tq), lambda b, h, i: (h, b * nq + i)),
            pl.BlockSpec((HEAD_DIM, seq), lambda b, h, i: (n_heads + h, b)),
            pl.BlockSpec((HEAD_DIM, seq), lambda b, h, i: (2 * n_heads + h, b)),
            pl.BlockSpec((None, 1, seq), lambda b, h, i: (h, 0, b)),
        ],
        out_specs=pl.BlockSpec((HEAD_DIM, tq), lambda b, h, i: (h, b * nq + i)),
        out_shape=jax.ShapeDtypeStruct((d_model, tokens), BF16),
        scratch_shapes=[pltpu.VMEM((seq, 2 * HEAD_DIM), BF16)],
        compiler_params=_params(("arbitrary", "arbitrary", "arbitrary")),
        name="fox_attn",
    )(qkv, qkv, qkv, c3)


def _outproj_t_kernel(ot_ref, w_ref, x_ref, y_ref):
    y_ref[...] = x_ref[...] + lax.dot_general(ot_ref[...], w_ref[...], _TN, preferred_element_type=F32)


def _outproj_t(ot, w, x2, *, ts):
    tokens, d_model = x2.shape
    return pl.pallas_call(
        _outproj_t_kernel,
        grid=(tokens // ts,),
        in_specs=[
            pl.BlockSpec((d_model, ts), lambda i: (0, i)),
            pl.BlockSpec((d_model, d_model), lambda i: (0, 0)),
            pl.BlockSpec((ts, d_model), lambda i: (i, 0)),
        ],
        out_specs=pl.BlockSpec((ts, d_model), lambda i: (i, 0)),
        out_shape=jax.ShapeDtypeStruct((tokens, d_model), F32),
        compiler_params=_params(("arbitrary",)),
        name="fox_outproj",
    )(ot, w, x2)


def _mlp_kernel(x_ref, g_ref, w1_ref, w2_ref, y_ref, h_ref, acc_ref):
    f = pl.program_id(1)

    @pl.when(f == 0)
    def _():
        h_ref[...] = _rms_bf16(x_ref[...], g_ref[...])
        acc_ref[...] = jnp.zeros_like(acc_ref)

    a = jnp.dot(h_ref[...], w1_ref[...], preferred_element_type=F32)
    a = jnp.square(jnp.maximum(a, 0.0)).astype(BF16)
    acc_ref[...] += jnp.dot(a, w2_ref[...], preferred_element_type=F32)

    @pl.when(f == pl.num_programs(1) - 1)
    def _():
        y_ref[...] = x_ref[...] + acc_ref[...]


def _mlp(x2, g, w1, w2, *, tm, tf):
    tokens, d_model = x2.shape
    d_ff = w1.shape[1]
    return pl.pallas_call(
        _mlp_kernel,
        grid=(tokens // tm, d_ff // tf),
        in_specs=[
            pl.BlockSpec((tm, d_model), lambda i, f: (i, 0)),
            pl.BlockSpec((1, d_model), lambda i, f: (0, 0)),
            pl.BlockSpec((d_model, tf), lambda i, f: (0, f)),
            pl.BlockSpec((tf, d_model), lambda i, f: (f, 0)),
        ],
        out_specs=pl.BlockSpec((tm, d_model), lambda i, f: (i, 0)),
        out_shape=jax.ShapeDtypeStruct((tokens, d_model), F32),
        scratch_shapes=[pltpu.VMEM((tm, d_model), BF16), pltpu.VMEM((tm, d_model), F32)],
        compiler_params=_params(("arbitrary", "arbitrary")),
        name="mlp",
    )(x2, g, w1, w2)


_TWO_PI = 2.0 * math.pi
_PI2_A = 6.28125
_PI2_B = float((np.array(_TWO_PI - _PI2_A, np.float32).view(np.uint32) & np.uint32(0xFFFFF000)).view(np.float32))
_PI2_C = _TWO_PI - _PI2_A - _PI2_B


def _rope_tables(pos_row, inv_col):
    ang = pos_row.astype(F32) * inv_col
    k = jnp.floor(ang * (1.0 / _TWO_PI) + 0.5)
    r = ((ang - k * _PI2_A) - k * _PI2_B) - k * _PI2_C
    return jnp.cos(r), jnp.sin(r)


def _dil_inproj_kernel(x_ref, pos_ref, inv_ref, g_ref, wqk_ref, wv_ref, gq_ref, gk_ref,
                       q_ref, kt_ref, v_ref):
    h = _rms_bf16(x_ref[...], g_ref[...])
    tl, d_model = x_ref.shape
    n_heads = d_model // HEAD_DIM
    half = ROT_DIM // 2
    cos, sin = _rope_tables(pos_ref[...], inv_ref[...])
    for part, gain_ref, mult in ((0, gq_ref, LOG2E / math.sqrt(HEAD_DIM)), (1, gk_ref, 1.0)):
        rows = slice(part * d_model, (part + 1) * d_model)
        y = lax.dot_general(wqk_ref[rows, :], h, _NT, preferred_element_type=F32)
        y = _head_norm(y.reshape(n_heads, HEAD_DIM, tl), gain_ref[...] * mult)
        x1, x2 = y[:, 0:half], y[:, half:ROT_DIM]
        y = jnp.concatenate([x1 * cos - x2 * sin, x2 * cos + x1 * sin, y[:, ROT_DIM:]], axis=1)
        y = y.reshape(d_model, tl)
        if part == 0:
            q_ref[...] = y.T.astype(BF16)
        else:
            kt_ref[...] = y.astype(BF16)
    v_ref[...] = jnp.dot(h, wv_ref[...], preferred_element_type=F32).astype(BF16)


def _dil_inproj(x3, pos_perm, inv, g, wqk, wv, gq, gk, *, dil, tl):
    batch, sub_len, _ = x3.shape
    d_model = wv.shape[0]
    nl = sub_len // tl
    tokens = batch * dil * sub_len
    flat = lambda b, r, l: (b * dil + r) * nl + l
    const = lambda b, r, l: (0, 0)
    return pl.pallas_call(
        _dil_inproj_kernel,
        grid=(batch, dil, nl),
        in_specs=[
            pl.BlockSpec((None, tl, d_model), lambda b, r, l: (b, l, r)),
            pl.BlockSpec((1, tl), lambda b, r, l: (0, flat(b, r, l))),
            pl.BlockSpec((ROT_DIM // 2, 1), const),
            pl.BlockSpec((1, d_model), const),
            pl.BlockSpec((2 * d_model, d_model), const),
            pl.BlockSpec((d_model, d_model), const),
            pl.BlockSpec((HEAD_DIM, 1), const),
            pl.BlockSpec((HEAD_DIM, 1), const),
        ],
        out_specs=[
            pl.BlockSpec((tl, d_model), lambda b, r, l: (flat(b, r, l), 0)),
            pl.BlockSpec((d_model, tl), lambda b, r, l: (0, flat(b, r, l))),
            pl.BlockSpec((tl, d_model), lambda b, r, l: (flat(b, r, l), 0)),
        ],
        out_shape=[
            jax.ShapeDtypeStruct((tokens, d_model), BF16),
            jax.ShapeDtypeStruct((d_model, tokens), BF16),
            jax.ShapeDtypeStruct((tokens, d_model), BF16),
        ],
        compiler_params=_params(("arbitrary", "arbitrary", "arbitrary")),
        name=f"dil_inproj_d{dil}",
    )(x3, pos_perm, inv, g, wqk, wv, gq, gk)


def _dil_attn_kernel(q_ref, ktp_ref, ktc_ref, vp_ref, vc_ref, o_ref, lse_ref, *, tq):
    i = pl.program_id(1)
    blk = DIL_BLOCK
    has_prev = i > 0
    d_model = q_ref.shape[1]
    kt_all = jnp.concatenate([ktp_ref[...], ktc_ref[...]], axis=1)
    v_all = jnp.concatenate([vp_ref[...], vc_ref[...]], axis=0)
    r = lax.broadcasted_iota(jnp.int32, (blk, 2 * blk), 0)
    c = lax.broadcasted_iota(jnp.int32, (blk, 2 * blk), 1)
    band = (c >= r) & (c <= r + blk)
    first_mask = band & (has_prev | (c >= blk))
    lane = lax.broadcasted_iota(jnp.int32, (blk, 2 * HEAD_DIM), 1)
    low = lane < HEAD_DIM
    for u in range(tq // blk):
        rows = slice(u * blk, (u + 1) * blk)
        mask = first_mask if u == 0 else band
        lse_tile = jnp.zeros((blk, 2 * HEAD_DIM), F32)
        for pair in range(d_model // (2 * HEAD_DIM)):
            cols = slice(pair * 2 * HEAD_DIM, (pair + 1) * 2 * HEAD_DIM)
            qp = q_ref[rows, cols]
            kt = kt_all[cols, u * blk:(u + 2) * blk]
            vv = v_all[u * blk:(u + 2) * blk, cols]
            outs = []
            for hh, sel in ((0, low), (1, ~low)):
                qh = jnp.where(sel, qp, jnp.zeros_like(qp))
                s = jnp.dot(qh, kt, preferred_element_type=F32)
                s = jnp.where(mask, s, NEG_INF)
                m = jnp.max(s, axis=1, keepdims=True)
                p = jnp.exp2(s - m)
                den = jnp.sum(p, axis=1, keepdims=True)
                outs.append(jnp.dot(p.astype(BF16), vv, preferred_element_type=F32) / den)
                lse_tile = lse_tile + jnp.where(lane == 2 * pair + hh, m + jnp.log2(den), 0.0)
            o_ref[rows, cols] = jnp.where(low, outs[0], outs[1]).astype(BF16)
        lse_ref[rows, :] = lse_tile


def _dil_attn(q, kt, v, *, batch, dil, sub_len, tq):
    tokens, d_model = q.shape
    nq = sub_len // tq
    per = tq // DIL_BLOCK
    first = lambda s, i: jnp.maximum((s * nq + i) * per - 1, 0)
    return pl.pallas_call(
        functools.partial(_dil_attn_kernel, tq=tq),
        grid=(batch * dil, nq),
        in_specs=[
            pl.BlockSpec((tq, d_model), lambda s, i: (s * nq + i, 0)),
            pl.BlockSpec((d_model, DIL_BLOCK), lambda s, i: (0, first(s, i))),
            pl.BlockSpec((d_model, tq), lambda s, i: (0, s * nq + i)),
            pl.BlockSpec((DIL_BLOCK, d_model), lambda s, i: (first(s, i), 0)),
            pl.BlockSpec((tq, d_model), lambda s, i: (s * nq + i, 0)),
        ],
        out_specs=[
            pl.BlockSpec((None, tq, d_model), lambda s, i: (s // dil, i, s % dil)),
            pl.BlockSpec((None, tq, 2 * HEAD_DIM), lambda s, i: (s // dil, i, s % dil)),
        ],
        out_shape=[
            jax.ShapeDtypeStruct((batch, sub_len, dil * d_model), BF16),
            jax.ShapeDtypeStruct((batch, sub_len, dil * 2 * HEAD_DIM), F32),
        ],
        compiler_params=_params(("arbitrary", "arbitrary")),
        name=f"dil_attn_d{dil}",
    )(q, kt, kt, v, v)


def _dil_out_kernel(o0_ref, o1_ref, o2_ref, l0_ref, l1_ref, l2_ref, e_ref, w_ref, x_ref, y_ref):
    lses = [l0_ref[...], l1_ref[...], l2_ref[...]]
    top = jnp.maximum(jnp.maximum(lses[0], lses[1]), lses[2])
    es = [jnp.exp2(l - top) for l in lses]
    den = es[0] + es[1] + es[2]
    merged = None
    for e, o_ref in zip(es, (o0_ref, o1_ref, o2_ref)):
        w = e / den
        hi = w.astype(BF16)
        lo = (w - hi.astype(F32)).astype(BF16)
        wfull = jnp.dot(jnp.concatenate([hi, lo], axis=1), e_ref[...], preferred_element_type=F32)
        term = wfull * o_ref[...].astype(F32)
        merged = term if merged is None else merged + term
    y_ref[...] = x_ref[...] + jnp.dot(merged.astype(BF16), w_ref[...], preferred_element_type=F32)


def _dil_out(os_, ls_, expand, w, x2, *, ts):
    tokens, d_model = x2.shape
    row = lambda i: (i, 0)
    const = lambda i: (0, 0)
    return pl.pallas_call(
        _dil_out_kernel,
        grid=(tokens // ts,),
        in_specs=[pl.BlockSpec((ts, d_model), row)] * 3 + [pl.BlockSpec((ts, 2 * HEAD_DIM), row)] * 3 + [
            pl.BlockSpec((4 * HEAD_DIM, d_model), const),
            pl.BlockSpec((d_model, d_model), const),
            pl.BlockSpec((ts, d_model), row),
        ],
        out_specs=pl.BlockSpec((ts, d_model), row),
        out_shape=jax.ShapeDtypeStruct((tokens, d_model), F32),
        compiler_params=_params(("arbitrary",)),
        name="dil_outproj",
    )(*os_, *ls_, expand, w, x2)


def _col(v):
    return v.reshape(-1, 1).astype(F32)


def _fox_layer(x2, norm_g, w_in, b_f, q_gain, k_gain, w_out, *, batch, seq, ts, tq):
    d_model = x2.shape[1]
    n_heads = d_model // HEAD_DIM
    wt = w_in[:, :3 * d_model].T.astype(BF16)
    wf = w_in[:, 3 * d_model:].T.astype(BF16)
    qkv, c = _fox_inproj(x2, norm_g.reshape(1, -1), wt, wf, _col(b_f), _col(q_gain), _col(k_gain),
                         seq=seq, ts=ts)
    ot = _fox_attn(qkv, c.reshape(n_heads, 1, -1), batch=batch, seq=seq, tq=tq)
    return _outproj_t(ot, w_out.astype(BF16), x2, ts=ts)


def _dil_layer(x2, positions, norm_g, w_in, q_gain, k_gain, w_out, *, batch, seq, tl, tq, ts):
    d_model = x2.shape[1]
    n_heads = d_model // HEAD_DIM
    inv = (ROPE_THETA ** (-np.arange(ROT_DIM // 2, dtype=np.float64) * 2.0 / ROT_DIM)).astype(np.float32)
    inv = jnp.asarray(inv).reshape(-1, 1)
    outs, lses = [], []
    for g, (window, dil) in enumerate(DIL_GROUPS):
        assert window // dil == DIL_BLOCK
        sub_len = seq // dil
        base = g * 3 * d_model
        wqk = w_in[:, base:base + 2 * d_model].T.astype(BF16)
        wv = w_in[:, base + 2 * d_model:base + 3 * d_model].astype(BF16)
        pos_perm = positions.reshape(batch, sub_len, dil).transpose(0, 2, 1).reshape(1, -1)
        x3 = x2.reshape(batch, sub_len, dil * d_model)
        q, kt, v = _dil_inproj(x3, pos_perm, inv, norm_g.reshape(1, -1), wqk, wv,
                               _col(q_gain[g]), _col(k_gain[g]), dil=dil, tl=min(tl, sub_len))
        o, lse = _dil_attn(q, kt, v, batch=batch, dil=dil, sub_len=sub_len, tq=min(tq, sub_len))
        outs.append(o.reshape(batch * seq, d_model))
        lses.append(lse.reshape(batch * seq, 2 * HEAD_DIM))
    head_of_col = np.arange(d_model) // HEAD_DIM
    expand = (np.arange(2 * HEAD_DIM)[:, None] == head_of_col[None, :]).astype(np.float32)
    expand = jnp.asarray(np.concatenate([expand, expand], axis=0), BF16)
    assert n_heads <= 2 * HEAD_DIM
    return _dil_out(outs, lses, expand, w_out.astype(BF16), x2, ts=ts)


def _forward(x, positions, fox_norm, fox_w_in, fox_b_f, fox_q_gain, fox_k_gain, fox_w_out,
             dil_norm, dil_w_in, dil_q_gain, dil_k_gain, dil_w_out, mlp_norm, mlp_w1, mlp_w2,
             *, ts, tq_fox, tl, tq_dil, tm, tf):
    batch, seq, d_model = x.shape
    x2 = x.reshape(batch * seq, d_model)
    depth = mlp_norm.shape[0]
    for layer in range(depth):
        j = layer // 2
        if layer % 2 == 0:
            x2 = _fox_layer(x2, fox_norm[j], fox_w_in[j], fox_b_f[j], fox_q_gain[j], fox_k_gain[j],
                            fox_w_out[j], batch=batch, seq=seq, ts=ts, tq=tq_fox)
        else:
            x2 = _dil_layer(x2, positions, dil_norm[j], dil_w_in[j], dil_q_gain[j], dil_k_gain[j],
                            dil_w_out[j], batch=batch, seq=seq, tl=tl, tq=tq_dil, ts=ts)
        x2 = _mlp(x2, mlp_norm[layer].reshape(1, -1), mlp_w1[layer].astype(BF16),
                  mlp_w2[layer].astype(BF16), tm=tm, tf=tf)
    return x2.reshape(batch, seq, d_model)


def kernel(x, positions, fox_norm, fox_w_in, fox_b_f, fox_q_gain, fox_k_gain, fox_w_out, dil_norm, dil_w_in, dil_q_gain, dil_k_gain, dil_w_out, mlp_norm, mlp_w1, mlp_w2):
    return _forward(x, positions, fox_norm, fox_w_in, fox_b_f, fox_q_gain, fox_k_gain, fox_w_out,
                    dil_norm, dil_w_in, dil_q_gain, dil_k_gain, dil_w_out, mlp_norm, mlp_w1, mlp_w2,
                    ts=512, tq_fox=512, tl=512, tq_dil=256, tm=1024, tf=1024)
```

```python
import functools
import math

import jax
import jax.numpy as jnp
import numpy as np
from jax import lax
from jax.experimental import pallas as pl
from jax.experimental.pallas import tpu as pltpu

F32, BF16 = jnp.float32, jnp.bfloat16
HEAD_DIM = 64
ROT_DIM = 16
ROPE_THETA = 500000.0
DIL_GROUPS = ((128, 1), (512, 4), (2048, 16))
DIL_BLOCK = 128
EPS = 1e-6
NEG_INF = -1e30
LOG2E = 1.4426950408889634
BIAS_ROWS = 3
FOX_SKIP_LOG2 = 160.0
FOX_MAX_LOOSENESS_LOG2 = 100.0
VMEM_LIMIT = 56 * 1024 * 1024

_NT = (((1,), (1,)), ((), ()))
_TN = (((0,), (0,)), ((), ()))


def _params(sem):
    return pltpu.CompilerParams(dimension_semantics=sem, vmem_limit_bytes=VMEM_LIMIT)


def _rms_bf16(x, g):
    return (x * lax.rsqrt(jnp.mean(x * x, axis=-1, keepdims=True) + EPS) * g).astype(BF16)


def _head_norm(y, gain):
    inv = lax.rsqrt(jnp.mean(y * y, axis=1, keepdims=True) + EPS)
    return y * inv * gain[None]


def _fox_inproj_kernel(x_ref, g_ref, wt_ref, wf_ref, bf_ref, gq_ref, gk_ref,
                       qkv_ref, c_ref, carry_ref, *, tiles_per_batch):
    i = pl.program_id(0)

    @pl.when(i % tiles_per_batch == 0)
    def _():
        carry_ref[...] = jnp.zeros_like(carry_ref)

    h = _rms_bf16(x_ref[...], g_ref[...])
    ts, d_model = x_ref.shape
    n_heads = d_model // HEAD_DIM
    for part, gain_ref, mult in ((0, gq_ref, LOG2E / math.sqrt(HEAD_DIM)), (1, gk_ref, 1.0)):
        rows = slice(part * d_model, (part + 1) * d_model)
        y = lax.dot_general(wt_ref[rows, :], h, _NT, preferred_element_type=F32)
        y = _head_norm(y.reshape(n_heads, HEAD_DIM, ts), gain_ref[...] * mult)
        qkv_ref[rows, :] = y.reshape(d_model, ts).astype(BF16)
    rows = slice(2 * d_model, 3 * d_model)
    qkv_ref[rows, :] = lax.dot_general(wt_ref[rows, :], h, _NT, preferred_element_type=F32).astype(BF16)

    f = lax.dot_general(wf_ref[...], h, _NT, preferred_element_type=F32) + bf_ref[...]
    lf = (jnp.minimum(f, 0.0) - jnp.log1p(jnp.exp(-jnp.abs(f)))) * LOG2E
    lane = lax.broadcasted_iota(jnp.int32, lf.shape, 1)
    shift = 1
    while shift < ts:
        lf = lf + jnp.where(lane >= shift, pltpu.roll(lf, shift, 1), 0.0)
        shift *= 2
    c = lf + carry_ref[:, 0:1]
    c_ref[...] = c
    carry_ref[...] = jnp.broadcast_to(c[:, ts - 1:ts], carry_ref.shape)


def _fox_inproj(x2, g, wt, wf, bf, gq, gk, *, seq, ts):
    tokens, d_model = x2.shape
    n_heads = d_model // HEAD_DIM
    const = lambda i: (0, 0)
    return pl.pallas_call(
        functools.partial(_fox_inproj_kernel, tiles_per_batch=seq // ts),
        grid=(tokens // ts,),
        in_specs=[
            pl.BlockSpec((ts, d_model), lambda i: (i, 0)),
            pl.BlockSpec((1, d_model), const),
            pl.BlockSpec((3 * d_model, d_model), const),
            pl.BlockSpec((n_heads, d_model), const),
            pl.BlockSpec((n_heads, 1), const),
            pl.BlockSpec((HEAD_DIM, 1), const),
            pl.BlockSpec((HEAD_DIM, 1), const),
        ],
        out_specs=[
            pl.BlockSpec((3 * d_model, ts), lambda i: (0, i)),
            pl.BlockSpec((n_heads, ts), lambda i: (0, i)),
        ],
        out_shape=[
            jax.ShapeDtypeStruct((3 * d_model, tokens), BF16),
            jax.ShapeDtypeStruct((n_heads, tokens), F32),
        ],
        scratch_shapes=[pltpu.VMEM((n_heads, 128), F32)],
        compiler_params=_params(("arbitrary",)),
        name="fox_inproj",
    )(x2, g, wt, wf, bf, gq, gk)


def _fox_attn_kernel(q_ref, k_ref, v_ref, c_ref, o_ref, kp_ref, *, tq):
    i = pl.program_id(2)
    seq = k_ref.shape[1]

    @pl.when(i == 0)
    def _build_keys():
        for n in range(seq // tq):
            sl = slice(n * tq, (n + 1) * tq)
            negc = -c_ref[:, sl]
            hi = negc.astype(BF16).astype(F32)
            rem = negc - hi
            mid = rem.astype(BF16).astype(F32)
            lo = rem - mid
            pad = jnp.zeros((HEAD_DIM - BIAS_ROWS, tq), F32)
            blk = jnp.concatenate([k_ref[:, sl].astype(F32), hi, mid, lo, pad], axis=0)
            kp_ref[sl, :] = blk.T.astype(BF16)

    ones_rows = (lax.broadcasted_iota(jnp.int32, (HEAD_DIM, tq), 0) < BIAS_ROWS).astype(BF16)
    qp = jnp.concatenate([q_ref[...], ones_rows], axis=0)

    def step(j, carry, masked):
        m, l, acc = carry
        off = pl.multiple_of(j * tq, tq)
        s = jnp.dot(kp_ref[pl.ds(off, tq), :], qp, preferred_element_type=F32)
        if masked:
            key = lax.broadcasted_iota(jnp.int32, s.shape, 0)
            qry = lax.broadcasted_iota(jnp.int32, s.shape, 1)
            s = jnp.where(key <= qry, s, NEG_INF)
        m_new = jnp.maximum(m, jnp.max(s, axis=0, keepdims=True))
        alpha = jnp.exp2(m - m_new)
        p = jnp.exp2(s - m_new)
        l = alpha * l + jnp.sum(p, axis=0, keepdims=True)
        pv = jnp.dot(v_ref[:, pl.ds(off, tq)], p.astype(BF16), preferred_element_type=F32)
        return m_new, l, alpha * acc + pv

    init = (jnp.full((1, tq), NEG_INF, F32), jnp.zeros((1, tq), F32), jnp.zeros((HEAD_DIM, tq), F32))
    carry = lax.fori_loop(0, i, lambda j, c: step(j, c, False), init)
    m, l, acc = step(i, carry, True)
    o_ref[...] = (acc / l).astype(BF16)


def _split3(x):
    hi = x.astype(BF16).astype(F32)
    rem = x - hi
    mid = rem.astype(BF16).astype(F32)
    return hi, mid, rem - mid


def _fox_fast_kernel(jlo_ref, kb_ref, q_ref, k_ref, v_ref, c_ref, o_ref, kp_ref, *s_refs, tq, tk, unroll):
    bh = pl.program_id(0) * pl.num_programs(1) + pl.program_id(1)
    seq = k_ref.shape[1]
    nq, per, nkv = seq // tq, tq // tk, seq // tk
    kb = kb_ref[0]
    pad_rows = HEAD_DIM - 2 * BIAS_ROWS

    for n in range(nq):
        sl = slice(n * tq, (n + 1) * tq)
        pieces = _split3(-c_ref[:, sl])
        blk = jnp.concatenate([k_ref[:, sl].astype(F32), *pieces, jnp.ones((BIAS_ROWS, tq), F32),
                               jnp.zeros((pad_rows, tq), F32)], axis=0)
        kp_ref[sl, :] = blk.T.astype(BF16)
    lane = lax.broadcasted_iota(jnp.int32, (tk, 2 * HEAD_DIM), 1)
    kp_ref[seq:seq + tk, :] = jnp.where(lane == HEAD_DIM, NEG_INF, 0.0).astype(BF16)

    ones_row = (lax.broadcasted_iota(jnp.int32, (16, tk), 0) == 0).astype(BF16)
    key = lax.broadcasted_iota(jnp.int32, (tk, tq), 0)
    qry = lax.broadcasted_iota(jnp.int32, (tk, tq), 1)

    def q_block(i, _):
        qoff = pl.multiple_of(i * tq, tq)
        qt = q_ref[:, pl.ds(qoff, tq)].astype(F32)
        qn = jnp.sqrt(jnp.sum(qt * qt, axis=0, keepdims=True))
        neg_m = c_ref[:, pl.ds(qoff, tq)] - qn * kb
        qp = jnp.concatenate([qt, jnp.ones((BIAS_ROWS, tq), F32), *_split3(neg_m),
                              jnp.zeros((pad_rows, tq), F32)], axis=0).astype(BF16)
        jlo = jlo_ref[bh * nq + i]
        n_off = i * per - jlo
        n_it = (n_off + unroll - 1) // unroll

        def qk(blk):
            off = pl.multiple_of(blk * tk, tk)
            return jnp.dot(kp_ref[pl.ds(off, tk), :], qp, preferred_element_type=F32)

        def pv(blk, s, acc):
            off = pl.multiple_of(blk * tk, tk)
            p = jnp.exp2(s).astype(BF16)
            vv = jnp.concatenate([v_ref[:, pl.ds(off, tk)], ones_row], axis=0)
            return acc + jnp.dot(vv, p, preferred_element_type=F32)

        def off_tile(g):
            return jnp.where(g < n_off, jlo + g, nkv), jnp.where(g < n_off, jlo + g, 0)

        def consumed_v(g):
            return jnp.where(g < per, i * per + g, off_tile(g - per)[1])

        for t in range(per):
            s = qk(i * per + t)
            s_refs[t][...] = jnp.where(key + t * tk <= qry, s, NEG_INF)

        def body(u, acc):
            pend = [s_refs[t][...] for t in range(per)]
            for t in range(unroll):
                g = u * unroll + t
                pend.append(qk(off_tile(g)[0]))
                acc = pv(consumed_v(g), pend.pop(0), acc)
            for t in range(per):
                s_refs[t][...] = pend[t]
            return acc

        acc = lax.fori_loop(0, n_it, body, jnp.zeros((HEAD_DIM + 16, tq), F32))
        for t in range(per):
            acc = pv(consumed_v(n_it * unroll + t), s_refs[t][...], acc)
        o_ref[:, pl.ds(qoff, tq)] = (acc[0:HEAD_DIM] / acc[HEAD_DIM:HEAD_DIM + 1]).astype(BF16)
        return 0

    lax.fori_loop(0, nq, q_block, 0)


def _fox_attn_fast(qkv, c3, jlo, kb, *, batch, seq, tq, tk, unroll):
    d_model = qkv.shape[0] // 3
    n_heads = d_model // HEAD_DIM
    tokens = qkv.shape[1]
    grid_spec = pltpu.PrefetchScalarGridSpec(
        num_scalar_prefetch=1,
        grid=(batch, n_heads),
        in_specs=[
            pl.BlockSpec(memory_space=pltpu.SMEM),
            pl.BlockSpec((HEAD_DIM, seq), lambda b, h, jlo: (h, b)),
            pl.BlockSpec((HEAD_DIM, seq), lambda b, h, jlo: (n_heads + h, b)),
            pl.BlockSpec((HEAD_DIM, seq), lambda b, h, jlo: (2 * n_heads + h, b)),
            pl.BlockSpec((None, 1, seq), lambda b, h, jlo: (h, 0, b)),
        ],
        out_specs=pl.BlockSpec((HEAD_DIM, seq), lambda b, h, jlo: (h, b)),
        scratch_shapes=[pltpu.VMEM((seq + tk, 2 * HEAD_DIM), BF16)] + [pltpu.VMEM((tk, tq), F32)] * (tq // tk),
    )
    return pl.pallas_call(
        functools.partial(_fox_fast_kernel, tq=tq, tk=tk, unroll=unroll),
        grid_spec=grid_spec,
        out_shape=jax.ShapeDtypeStruct((d_model, tokens), BF16),
        compiler_params=_params(("arbitrary", "arbitrary")),
        name="fox_attn_fast",
    )(jlo, kb, qkv, qkv, qkv, c3)


def _fox_attn_exact(qkv, c3, *, batch, seq, tq):
    d_model = qkv.shape[0] // 3
    n_heads = d_model // HEAD_DIM
    tokens = qkv.shape[1]
    nq = seq // tq
    return pl.pallas_call(
        functools.partial(_fox_attn_kernel, tq=tq),
        grid=(batch, n_heads, nq),
        in_specs=[
            pl.BlockSpec((HEAD_DIM, tq), lambda b, h, i: (h, b * nq + i)),
            pl.BlockSpec((HEAD_DIM, seq), lambda b, h, i: (n_heads + h, b)),
            pl.BlockSpec((HEAD_DIM, seq), lambda b, h, i: (2 * n_heads + h, b)),
            pl.BlockSpec((None, 1, seq), lambda b, h, i: (h, 0, b)),
        ],
        out_specs=pl.BlockSpec((HEAD_DIM, tq), lambda b, h, i: (h, b * nq + i)),
        out_shape=jax.ShapeDtypeStruct((d_model, tokens), BF16),
        scratch_shapes=[pltpu.VMEM((seq, 2 * HEAD_DIM), BF16)],
        compiler_params=_params(("arbitrary", "arbitrary", "arbitrary")),
        name="fox_attn",
    )(qkv, qkv, qkv, c3)


def _outproj_t_kernel(ot_ref, w_ref, x_ref, y_ref):
    y_ref[...] = x_ref[...] + lax.dot_general(ot_ref[...], w_ref[...], _TN, preferred_element_type=F32)


def _outproj_t(ot, w, x2, *, ts):
    tokens, d_model = x2.shape
    return pl.pallas_call(
        _outproj_t_kernel,
        grid=(tokens // ts,),
        in_specs=[
            pl.BlockSpec((d_model, ts), lambda i: (0, i)),
            pl.BlockSpec((d_model, d_model), lambda i: (0, 0)),
            pl.BlockSpec((ts, d_model), lambda i: (i, 0)),
        ],
        out_specs=pl.BlockSpec((ts, d_model), lambda i: (i, 0)),
        out_shape=jax.ShapeDtypeStruct((tokens, d_model), F32),
        compiler_params=_params(("arbitrary",)),
        name="fox_outproj",
    )(ot, w, x2)


def _mlp_kernel(x_ref, g_ref, w1_ref, w2_ref, y_ref, h_ref, acc_ref):
    f = pl.program_id(1)

    @pl.when(f == 0)
    def _():
        h_ref[...] = _rms_bf16(x_ref[...], g_ref[...])
        acc_ref[...] = jnp.zeros_like(acc_ref)

    a = jnp.dot(h_ref[...], w1_ref[...], preferred_element_type=F32)
    a = jnp.square(jnp.maximum(a, 0.0)).astype(BF16)
    acc_ref[...] += jnp.dot(a, w2_ref[...], preferred_element_type=F32)

    @pl.when(f == pl.num_programs(1) - 1)
    def _():
        y_ref[...] = x_ref[...] + acc_ref[...]


def _mlp(x2, g, w1, w2, *, tm, tf):
    tokens, d_model = x2.shape
    d_ff = w1.shape[1]
    return pl.pallas_call(
        _mlp_kernel,
        grid=(tokens // tm, d_ff // tf),
        in_specs=[
            pl.BlockSpec((tm, d_model), lambda i, f: (i, 0)),
            pl.BlockSpec((1, d_model), lambda i, f: (0, 0)),
            pl.BlockSpec((d_model, tf), lambda i, f: (0, f)),
            pl.BlockSpec((tf, d_model), lambda i, f: (f, 0)),
        ],
        out_specs=pl.BlockSpec((tm, d_model), lambda i, f: (i, 0)),
        out_shape=jax.ShapeDtypeStruct((tokens, d_model), F32),
        scratch_shapes=[pltpu.VMEM((tm, d_model), BF16), pltpu.VMEM((tm, d_model), F32)],
        compiler_params=_params(("arbitrary", "arbitrary")),
        name="mlp",
    )(x2, g, w1, w2)


_TWO_PI = 2.0 * math.pi
_PI2_A = 6.28125
_PI2_B = float((np.array(_TWO_PI - _PI2_A, np.float32).view(np.uint32) & np.uint32(0xFFFFF000)).view(np.float32))
_PI2_C = _TWO_PI - _PI2_A - _PI2_B


def _rope_tables(pos_row, inv_col):
    ang = pos_row.astype(F32) * inv_col
    k = jnp.floor(ang * (1.0 / _TWO_PI) + 0.5)
    r = ((ang - k * _PI2_A) - k * _PI2_B) - k * _PI2_C
    return jnp.cos(r), jnp.sin(r)


def _dil_inproj_kernel(x_ref, pos_ref, inv_ref, g_ref, wqk_ref, wv_ref, gq_ref, gk_ref,
                       q_ref, kt_ref, v_ref):
    h = _rms_bf16(x_ref[...], g_ref[...])
    tl, d_model = x_ref.shape
    n_heads = d_model // HEAD_DIM
    half = ROT_DIM // 2
    cos, sin = _rope_tables(pos_ref[...], inv_ref[...])
    for part, gain_ref, mult in ((0, gq_ref, LOG2E / math.sqrt(HEAD_DIM)), (1, gk_ref, 1.0)):
        rows = slice(part * d_model, (part + 1) * d_model)
        y = lax.dot_general(wqk_ref[rows, :], h, _NT, preferred_element_type=F32)
        y = _head_norm(y.reshape(n_heads, HEAD_DIM, tl), gain_ref[...] * mult)
        x1, x2 = y[:, 0:half], y[:, half:ROT_DIM]
        y = jnp.concatenate([x1 * cos - x2 * sin, x2 * cos + x1 * sin, y[:, ROT_DIM:]], axis=1)
        y = y.reshape(d_model, tl)
        if part == 0:
            q_ref[...] = y.T.astype(BF16)
        else:
            kt_ref[...] = y.astype(BF16)
    v_ref[...] = jnp.dot(h, wv_ref[...], preferred_element_type=F32).astype(BF16)


def _dil_inproj(x3, pos_perm, inv, g, wqk, wv, gq, gk, *, dil, tl):
    batch, sub_len, _ = x3.shape
    d_model = wv.shape[0]
    nl = sub_len // tl
    tokens = batch * dil * sub_len
    flat = lambda b, r, l: (b * dil + r) * nl + l
    const = lambda b, r, l: (0, 0)
    return pl.pallas_call(
        _dil_inproj_kernel,
        grid=(batch, dil, nl),
        in_specs=[
            pl.BlockSpec((None, tl, d_model), lambda b, r, l: (b, l, r)),
            pl.BlockSpec((1, tl), lambda b, r, l: (0, flat(b, r, l))),
            pl.BlockSpec((ROT_DIM // 2, 1), const),
            pl.BlockSpec((1, d_model), const),
            pl.BlockSpec((2 * d_model, d_model), const),
            pl.BlockSpec((d_model, d_model), const),
            pl.BlockSpec((HEAD_DIM, 1), const),
            pl.BlockSpec((HEAD_DIM, 1), const),
        ],
        out_specs=[
            pl.BlockSpec((tl, d_model), lambda b, r, l: (flat(b, r, l), 0)),
            pl.BlockSpec((d_model, tl), lambda b, r, l: (0, flat(b, r, l))),
            pl.BlockSpec((tl, d_model), lambda b, r, l: (flat(b, r, l), 0)),
        ],
        out_shape=[
            jax.ShapeDtypeStruct((tokens, d_model), BF16),
            jax.ShapeDtypeStruct((d_model, tokens), BF16),
            jax.ShapeDtypeStruct((tokens, d_model), BF16),
        ],
        compiler_params=_params(("arbitrary", "arbitrary", "arbitrary")),
        name=f"dil_inproj_d{dil}",
    )(x3, pos_perm, inv, g, wqk, wv, gq, gk)


def _dil_attn_kernel(q_ref, ktp_ref, ktc_ref, vp_ref, vc_ref, o_ref, lse_ref, *, tq):
    i = pl.program_id(1)
    blk = DIL_BLOCK
    has_prev = i > 0
    d_model = q_ref.shape[1]
    kt_all = jnp.concatenate([ktp_ref[...], ktc_ref[...]], axis=1)
    v_all = jnp.concatenate([vp_ref[...], vc_ref[...]], axis=0)
    r = lax.broadcasted_iota(jnp.int32, (blk, 2 * blk), 0)
    c = lax.broadcasted_iota(jnp.int32, (blk, 2 * blk), 1)
    band = (c >= r) & (c <= r + blk)
    first_mask = band & (has_prev | (c >= blk))
    lane = lax.broadcasted_iota(jnp.int32, (blk, 2 * HEAD_DIM), 1)
    low = lane < HEAD_DIM
    for u in range(tq // blk):
        rows = slice(u * blk, (u + 1) * blk)
        mask = first_mask if u == 0 else band
        lse_tile = jnp.zeros((blk, 2 * HEAD_DIM), F32)
        for pair in range(d_model // (2 * HEAD_DIM)):
            cols = slice(pair * 2 * HEAD_DIM, (pair + 1) * 2 * HEAD_DIM)
            qp = q_ref[rows, cols]
            kt = kt_all[cols, u * blk:(u + 2) * blk]
            vv = v_all[u * blk:(u + 2) * blk, cols]
            outs = []
            for hh, sel in ((0, low), (1, ~low)):
                qh = jnp.where(sel, qp, jnp.zeros_like(qp))
                s = jnp.dot(qh, kt, preferred_element_type=F32)
                s = jnp.where(mask, s, NEG_INF)
                m = jnp.max(s, axis=1, keepdims=True)
                p = jnp.exp2(s - m)
                den = jnp.sum(p, axis=1, keepdims=True)
                outs.append(jnp.dot(p.astype(BF16), vv, preferred_element_type=F32) / den)
                lse_tile = lse_tile + jnp.where(lane == 2 * pair + hh, m + jnp.log2(den), 0.0)
            o_ref[rows, cols] = jnp.where(low, outs[0], outs[1]).astype(BF16)
        lse_ref[rows, :] = lse_tile


def _dil_attn(q, kt, v, *, batch, dil, sub_len, tq):
    tokens, d_model = q.shape
    nq = sub_len // tq
    per = tq // DIL_BLOCK
    first = lambda s, i: jnp.maximum((s * nq + i) * per - 1, 0)
    return pl.pallas_call(
        functools.partial(_dil_attn_kernel, tq=tq),
        grid=(batch * dil, nq),
        in_specs=[
            pl.BlockSpec((tq, d_model), lambda s, i: (s * nq + i, 0)),
            pl.BlockSpec((d_model, DIL_BLOCK), lambda s, i: (0, first(s, i))),
            pl.BlockSpec((d_model, tq), lambda s, i: (0, s * nq + i)),
            pl.BlockSpec((DIL_BLOCK, d_model), lambda s, i: (first(s, i), 0)),
            pl.BlockSpec((tq, d_model), lambda s, i: (s * nq + i, 0)),
        ],
        out_specs=[
            pl.BlockSpec((None, tq, d_model), lambda s, i: (s // dil, i, s % dil)),
            pl.BlockSpec((None, tq, 2 * HEAD_DIM), lambda s, i: (s // dil, i, s % dil)),
        ],
        out_shape=[
            jax.ShapeDtypeStruct((batch, sub_len, dil * d_model), BF16),
            jax.ShapeDtypeStruct((batch, sub_len, dil * 2 * HEAD_DIM), F32),
        ],
        compiler_params=_params(("arbitrary", "arbitrary")),
        name=f"dil_attn_d{dil}",
    )(q, kt, kt, v, v)


def _dil_out_kernel(o0_ref, o1_ref, o2_ref, l0_ref, l1_ref, l2_ref, e_ref, w_ref, x_ref, y_ref):
    lses = [l0_ref[...], l1_ref[...], l2_ref[...]]
    top = jnp.maximum(jnp.maximum(lses[0], lses[1]), lses[2])
    es = [jnp.exp2(l - top) for l in lses]
    den = es[0] + es[1] + es[2]
    merged = None
    for e, o_ref in zip(es, (o0_ref, o1_ref, o2_ref)):
        w = e / den
        hi = w.astype(BF16)
        lo = (w - hi.astype(F32)).astype(BF16)
        wfull = jnp.dot(jnp.concatenate([hi, lo], axis=1), e_ref[...], preferred_element_type=F32)
        term = wfull * o_ref[...].astype(F32)
        merged = term if merged is None else merged + term
    y_ref[...] = x_ref[...] + jnp.dot(merged.astype(BF16), w_ref[...], preferred_element_type=F32)


def _dil_out(os_, ls_, expand, w, x2, *, ts):
    tokens, d_model = x2.shape
    row = lambda i: (i, 0)
    const = lambda i: (0, 0)
    return pl.pallas_call(
        _dil_out_kernel,
        grid=(tokens // ts,),
        in_specs=[pl.BlockSpec((ts, d_model), row)] * 3 + [pl.BlockSpec((ts, 2 * HEAD_DIM), row)] * 3 + [
            pl.BlockSpec((4 * HEAD_DIM, d_model), const),
            pl.BlockSpec((d_model, d_model), const),
            pl.BlockSpec((ts, d_model), row),
        ],
        out_specs=pl.BlockSpec((ts, d_model), row),
        out_shape=jax.ShapeDtypeStruct((tokens, d_model), F32),
        compiler_params=_params(("arbitrary",)),
        name="dil_outproj",
    )(*os_, *ls_, expand, w, x2)


def _col(v):
    return v.reshape(-1, 1).astype(F32)


def _fox_layer(x2, norm_g, w_in, b_f, q_gain, k_gain, w_out, *, batch, seq, ts, tq, tk, unroll):
    d_model = x2.shape[1]
    n_heads = d_model // HEAD_DIM
    wt = w_in[:, :3 * d_model].T.astype(BF16)
    wf = w_in[:, 3 * d_model:].T.astype(BF16)
    qkv, c = _fox_inproj(x2, norm_g.reshape(1, -1), wt, wf, _col(b_f), _col(q_gain), _col(k_gain),
                         seq=seq, ts=ts)
    c3 = c.reshape(n_heads, 1, -1)

    slack = 1.0 + 2.0 ** -7
    qb = math.sqrt(HEAD_DIM) * jnp.max(jnp.abs(q_gain)) * (LOG2E / math.sqrt(HEAD_DIM)) * slack
    kb = math.sqrt(HEAD_DIM) * jnp.max(jnp.abs(k_gain)) * slack
    per = tq // tk
    ch = c.reshape(n_heads, batch, seq)
    gap = ch[:, :, ::tq, None] - ch[:, :, None, tk - 1::tk]
    below = jnp.arange(seq // tk)[None, :] < (jnp.arange(seq // tq) * per)[:, None]
    jlo = jnp.sum((gap < -FOX_SKIP_LOG2) & below, axis=-1, dtype=jnp.int32)
    jlo = jlo.transpose(1, 0, 2).reshape(-1)

    ot = lax.cond(
        2.0 * qb * kb <= FOX_MAX_LOOSENESS_LOG2,
        lambda: _fox_attn_fast(qkv, c3, jlo, kb.reshape(1).astype(F32), batch=batch, seq=seq,
                               tq=tq, tk=tk, unroll=unroll),
        lambda: _fox_attn_exact(qkv, c3, batch=batch, seq=seq, tq=tq),
    )
    return _outproj_t(ot, w_out.astype(BF16), x2, ts=ts)


def _dil_layer(x2, positions, norm_g, w_in, q_gain, k_gain, w_out, *, batch, seq, tl, tq, ts):
    d_model = x2.shape[1]
    n_heads = d_model // HEAD_DIM
    inv = (ROPE_THETA ** (-np.arange(ROT_DIM // 2, dtype=np.float64) * 2.0 / ROT_DIM)).astype(np.float32)
    inv = jnp.asarray(inv).reshape(-1, 1)
    outs, lses = [], []
    for g, (window, dil) in enumerate(DIL_GROUPS):
        assert window // dil == DIL_BLOCK
        sub_len = seq // dil
        base = g * 3 * d_model
        wqk = w_in[:, base:base + 2 * d_model].T.astype(BF16)
        wv = w_in[:, base + 2 * d_model:base + 3 * d_model].astype(BF16)
        pos_perm = positions.reshape(batch, sub_len, dil).transpose(0, 2, 1).reshape(1, -1)
        x3 = x2.reshape(batch, sub_len, dil * d_model)
        q, kt, v = _dil_inproj(x3, pos_perm, inv, norm_g.reshape(1, -1), wqk, wv,
                               _col(q_gain[g]), _col(k_gain[g]), dil=dil, tl=min(tl, sub_len))
        o, lse = _dil_attn(q, kt, v, batch=batch, dil=dil, sub_len=sub_len, tq=min(tq, sub_len))
        outs.append(o.reshape(batch * seq, d_model))
        lses.append(lse.reshape(batch * seq, 2 * HEAD_DIM))
    head_of_col = np.arange(d_model) // HEAD_DIM
    expand = (np.arange(2 * HEAD_DIM)[:, None] == head_of_col[None, :]).astype(np.float32)
    expand = jnp.asarray(np.concatenate([expand, expand], axis=0), BF16)
    assert n_heads <= 2 * HEAD_DIM
    return _dil_out(outs, lses, expand, w_out.astype(BF16), x2, ts=ts)


def _forward(x, positions, fox_norm, fox_w_in, fox_b_f, fox_q_gain, fox_k_gain, fox_w_out,
             dil_norm, dil_w_in, dil_q_gain, dil_k_gain, dil_w_out, mlp_norm, mlp_w1, mlp_w2,
             *, ts, tq_fox, tk_fox, unroll_fox, tl, tq_dil, tm, tf):
    batch, seq, d_model = x.shape
    x2 = x.reshape(batch * seq, d_model)
    depth = mlp_norm.shape[0]
    for layer in range(depth):
        j = layer // 2
        if layer % 2 == 0:
            x2 = _fox_layer(x2, fox_norm[j], fox_w_in[j], fox_b_f[j], fox_q_gain[j], fox_k_gain[j],
                            fox_w_out[j], batch=batch, seq=seq, ts=ts, tq=tq_fox, tk=tk_fox,
                            unroll=unroll_fox)
        else:
            x2 = _dil_layer(x2, positions, dil_norm[j], dil_w_in[j], dil_q_gain[j], dil_k_gain[j],
                            dil_w_out[j], batch=batch, seq=seq, tl=tl, tq=tq_dil, ts=ts)
        x2 = _mlp(x2, mlp_norm[layer].reshape(1, -1), mlp_w1[layer].astype(BF16),
                  mlp_w2[layer].astype(BF16), tm=tm, tf=tf)
    return x2.reshape(batch, seq, d_model)


def kernel(x, positions, fox_norm, fox_w_in, fox_b_f, fox_q_gain, fox_k_gain, fox_w_out, dil_norm, dil_w_in, dil_q_gain, dil_k_gain, dil_w_out, mlp_norm, mlp_w1, mlp_w2):
    return _forward(x, positions, fox_norm, fox_w_in, fox_b_f, fox_q_gain, fox_k_gain, fox_w_out,
                    dil_norm, dil_w_in, dil_q_gain, dil_k_gain, dil_w_out, mlp_norm, mlp_w1, mlp_w2,
                    ts=512, tq_fox=512, tk_fox=256, unroll_fox=6, tl=512, tq_dil=256, tm=1024, tf=1024)
```

```python
import functools
import math

import jax
import jax.numpy as jnp
import numpy as np
from jax import lax
from jax.experimental import pallas as pl
from jax.experimental.pallas import tpu as pltpu

F32, BF16 = jnp.float32, jnp.bfloat16
HEAD_DIM = 64
ROT_DIM = 16
ROPE_THETA = 500000.0
DIL_GROUPS = ((128, 1), (512, 4), (2048, 16))
DIL_BLOCK = 128
EPS = 1e-6
NEG_INF = -1e30
LOG2E = 1.4426950408889634
BIAS_ROWS = 3
FOX_SKIP_LOG2 = 160.0
FOX_MAX_LOOSENESS_LOG2 = 100.0
DIL_MAX_SCORE_LOG2 = 50.0
VMEM_LIMIT = 56 * 1024 * 1024

_NT = (((1,), (1,)), ((), ()))
_TN = (((0,), (0,)), ((), ()))


def _params(sem):
    return pltpu.CompilerParams(dimension_semantics=sem, vmem_limit_bytes=VMEM_LIMIT)


def _rms_bf16(x, g):
    return (x * lax.rsqrt(jnp.mean(x * x, axis=-1, keepdims=True) + EPS) * g).astype(BF16)


def _head_norm(y, gain):
    inv = lax.rsqrt(jnp.mean(y * y, axis=1, keepdims=True) + EPS)
    return y * inv * gain[None]


def _fox_inproj_kernel(x_ref, g_ref, wt_ref, wf_ref, bf_ref, gq_ref, gk_ref,
                       qkv_ref, c_ref, carry_ref, *, tiles_per_batch):
    i = pl.program_id(0)

    @pl.when(i % tiles_per_batch == 0)
    def _():
        carry_ref[...] = jnp.zeros_like(carry_ref)

    h = _rms_bf16(x_ref[...], g_ref[...])
    ts, d_model = x_ref.shape
    n_heads = d_model // HEAD_DIM

    f = lax.dot_general(wf_ref[...], h, _NT, preferred_element_type=F32) + bf_ref[...]
    lf = (jnp.minimum(f, 0.0) - jnp.log1p(jnp.exp(-jnp.abs(f)))) * LOG2E
    lane = lax.broadcasted_iota(jnp.int32, lf.shape, 1)
    shift = 1
    while shift < ts:
        lf = lf + jnp.where(lane >= shift, pltpu.roll(lf, shift, 1), 0.0)
        shift *= 2
    c = lf + carry_ref[:, 0:1]
    c_ref[...] = c
    carry_ref[...] = jnp.broadcast_to(c[:, ts - 1:ts], carry_ref.shape)

    for part, gain_ref, mult in ((0, gq_ref, LOG2E / math.sqrt(HEAD_DIM)), (1, gk_ref, 1.0)):
        rows = slice(part * d_model, (part + 1) * d_model)
        y = lax.dot_general(wt_ref[rows, :], h, _NT, preferred_element_type=F32)
        y = _head_norm(y.reshape(n_heads, HEAD_DIM, ts), gain_ref[...] * mult)
        qkv_ref[rows, :] = y.reshape(d_model, ts).astype(BF16)
    rows = slice(2 * d_model, 3 * d_model)
    qkv_ref[rows, :] = lax.dot_general(wt_ref[rows, :], h, _NT, preferred_element_type=F32).astype(BF16)


def _fox_inproj(x2, g, wt, wf, bf, gq, gk, *, seq, ts):
    tokens, d_model = x2.shape
    n_heads = d_model // HEAD_DIM
    const = lambda i: (0, 0)
    return pl.pallas_call(
        functools.partial(_fox_inproj_kernel, tiles_per_batch=seq // ts),
        grid=(tokens // ts,),
        in_specs=[
            pl.BlockSpec((ts, d_model), lambda i: (i, 0)),
            pl.BlockSpec((1, d_model), const),
            pl.BlockSpec((3 * d_model, d_model), const),
            pl.BlockSpec((n_heads, d_model), const),
            pl.BlockSpec((n_heads, 1), const),
            pl.BlockSpec((HEAD_DIM, 1), const),
            pl.BlockSpec((HEAD_DIM, 1), const),
        ],
        out_specs=[
            pl.BlockSpec((3 * d_model, ts), lambda i: (0, i)),
            pl.BlockSpec((n_heads, ts), lambda i: (0, i)),
        ],
        out_shape=[
            jax.ShapeDtypeStruct((3 * d_model, tokens), BF16),
            jax.ShapeDtypeStruct((n_heads, tokens), F32),
        ],
        scratch_shapes=[pltpu.VMEM((n_heads, 128), F32)],
        compiler_params=_params(("arbitrary",)),
        name="fox_inproj",
    )(x2, g, wt, wf, bf, gq, gk)


def _fox_attn_kernel(q_ref, k_ref, v_ref, c_ref, o_ref, kp_ref, *, tq):
    i = pl.program_id(2)
    seq = k_ref.shape[1]

    @pl.when(i == 0)
    def _build_keys():
        for n in range(seq // tq):
            sl = slice(n * tq, (n + 1) * tq)
            negc = -c_ref[:, sl]
            hi = negc.astype(BF16).astype(F32)
            rem = negc - hi
            mid = rem.astype(BF16).astype(F32)
            lo = rem - mid
            pad = jnp.zeros((HEAD_DIM - BIAS_ROWS, tq), F32)
            blk = jnp.concatenate([k_ref[:, sl].astype(F32), hi, mid, lo, pad], axis=0)
            kp_ref[sl, :] = blk.T.astype(BF16)

    ones_rows = (lax.broadcasted_iota(jnp.int32, (HEAD_DIM, tq), 0) < BIAS_ROWS).astype(BF16)
    qp = jnp.concatenate([q_ref[...], ones_rows], axis=0)

    def step(j, carry, masked):
        m, l, acc = carry
        off = pl.multiple_of(j * tq, tq)
        s = jnp.dot(kp_ref[pl.ds(off, tq), :], qp, preferred_element_type=F32)
        if masked:
            key = lax.broadcasted_iota(jnp.int32, s.shape, 0)
            qry = lax.broadcasted_iota(jnp.int32, s.shape, 1)
            s = jnp.where(key <= qry, s, NEG_INF)
        m_new = jnp.maximum(m, jnp.max(s, axis=0, keepdims=True))
        alpha = jnp.exp2(m - m_new)
        p = jnp.exp2(s - m_new)
        l = alpha * l + jnp.sum(p, axis=0, keepdims=True)
        pv = jnp.dot(v_ref[:, pl.ds(off, tq)], p.astype(BF16), preferred_element_type=F32)
        return m_new, l, alpha * acc + pv

    init = (jnp.full((1, tq), NEG_INF, F32), jnp.zeros((1, tq), F32), jnp.zeros((HEAD_DIM, tq), F32))
    carry = lax.fori_loop(0, i, lambda j, c: step(j, c, False), init)
    m, l, acc = step(i, carry, True)
    o_ref[...] = (acc / l).astype(BF16)


def _split3(x):
    hi = x.astype(BF16).astype(F32)
    rem = x - hi
    mid = rem.astype(BF16).astype(F32)
    return hi, mid, rem - mid


def _fox_fast_kernel(jlo_ref, kb_ref, q_ref, k_ref, v_ref, c_ref, o_ref, kp_ref, *s_refs, tq, tk, unroll):
    bh = pl.program_id(0) * pl.num_programs(1) + pl.program_id(1)
    seq = k_ref.shape[1]
    nq, per, nkv = seq // tq, tq // tk, seq // tk
    kb = kb_ref[0]
    pad_rows = HEAD_DIM - 2 * BIAS_ROWS

    for n in range(nq):
        sl = slice(n * tq, (n + 1) * tq)
        pieces = _split3(-c_ref[:, sl])
        blk = jnp.concatenate([k_ref[:, sl].astype(F32), *pieces, jnp.ones((BIAS_ROWS, tq), F32),
                               jnp.zeros((pad_rows, tq), F32)], axis=0)
        kp_ref[sl, :] = blk.T.astype(BF16)
    lane = lax.broadcasted_iota(jnp.int32, (tk, 2 * HEAD_DIM), 1)
    kp_ref[seq:seq + tk, :] = jnp.where(lane == HEAD_DIM, NEG_INF, 0.0).astype(BF16)

    ones_row = (lax.broadcasted_iota(jnp.int32, (16, tk), 0) == 0).astype(BF16)
    key = lax.broadcasted_iota(jnp.int32, (tk, tq), 0)
    qry = lax.broadcasted_iota(jnp.int32, (tk, tq), 1)

    def q_block(i, _):
        qoff = pl.multiple_of(i * tq, tq)
        qt = q_ref[:, pl.ds(qoff, tq)].astype(F32)
        qn = jnp.sqrt(jnp.sum(qt * qt, axis=0, keepdims=True))
        neg_m = c_ref[:, pl.ds(qoff, tq)] - qn * kb
        qp = jnp.concatenate([qt, jnp.ones((BIAS_ROWS, tq), F32), *_split3(neg_m),
                              jnp.zeros((pad_rows, tq), F32)], axis=0).astype(BF16)
        jlo = jlo_ref[bh * nq + i]
        n_off = i * per - jlo
        n_it = (n_off + unroll - 1) // unroll

        def qk(blk):
            off = pl.multiple_of(blk * tk, tk)
            return jnp.dot(kp_ref[pl.ds(off, tk), :], qp, preferred_element_type=F32)

        def pv(blk, s, acc):
            off = pl.multiple_of(blk * tk, tk)
            p = jnp.exp2(s).astype(BF16)
            vv = jnp.concatenate([v_ref[:, pl.ds(off, tk)], ones_row], axis=0)
            return acc + jnp.dot(vv, p, preferred_element_type=F32)

        def off_tile(g):
            return jnp.where(g < n_off, jlo + g, nkv), jnp.where(g < n_off, jlo + g, 0)

        def consumed_v(g):
            return jnp.where(g < per, i * per + g, off_tile(g - per)[1])

        for t in range(per):
            s = qk(i * per + t)
            s_refs[t][...] = jnp.where(key + t * tk <= qry, s, NEG_INF)

        def body(u, acc):
            pend = [s_refs[t][...] for t in range(per)]
            for t in range(unroll):
                g = u * unroll + t
                pend.append(qk(off_tile(g)[0]))
                acc = pv(consumed_v(g), pend.pop(0), acc)
            for t in range(per):
                s_refs[t][...] = pend[t]
            return acc

        acc = lax.fori_loop(0, n_it, body, jnp.zeros((HEAD_DIM + 16, tq), F32))
        for t in range(per):
            acc = pv(consumed_v(n_it * unroll + t), s_refs[t][...], acc)
        o_ref[:, pl.ds(qoff, tq)] = (acc[0:HEAD_DIM] / acc[HEAD_DIM:HEAD_DIM + 1]).astype(BF16)
        return 0

    lax.fori_loop(0, nq, q_block, 0)


def _fox_attn_fast(qkv, c3, jlo, kb, *, batch, seq, tq, tk, unroll):
    d_model = qkv.shape[0] // 3
    n_heads = d_model // HEAD_DIM
    tokens = qkv.shape[1]
    grid_spec = pltpu.PrefetchScalarGridSpec(
        num_scalar_prefetch=1,
        grid=(batch, n_heads),
        in_specs=[
            pl.BlockSpec(memory_space=pltpu.SMEM),
            pl.BlockSpec((HEAD_DIM, seq), lambda b, h, jlo: (h, b)),
            pl.BlockSpec((HEAD_DIM, seq), lambda b, h, jlo: (n_heads + h, b)),
            pl.BlockSpec((HEAD_DIM, seq), lambda b, h, jlo: (2 * n_heads + h, b)),
            pl.BlockSpec((None, 1, seq), lambda b, h, jlo: (h, 0, b)),
        ],
        out_specs=pl.BlockSpec((HEAD_DIM, seq), lambda b, h, jlo: (h, b)),
        scratch_shapes=[pltpu.VMEM((seq + tk, 2 * HEAD_DIM), BF16)] + [pltpu.VMEM((tk, tq), F32)] * (tq // tk),
    )
    return pl.pallas_call(
        functools.partial(_fox_fast_kernel, tq=tq, tk=tk, unroll=unroll),
        grid_spec=grid_spec,
        out_shape=jax.ShapeDtypeStruct((d_model, tokens), BF16),
        compiler_params=_params(("arbitrary", "arbitrary")),
        name="fox_attn_fast",
    )(jlo, kb, qkv, qkv, qkv, c3)


def _fox_attn_exact(qkv, c3, *, batch, seq, tq):
    d_model = qkv.shape[0] // 3
    n_heads = d_model // HEAD_DIM
    tokens = qkv.shape[1]
    nq = seq // tq
    return pl.pallas_call(
        functools.partial(_fox_attn_kernel, tq=tq),
        grid=(batch, n_heads, nq),
        in_specs=[
            pl.BlockSpec((HEAD_DIM, tq), lambda b, h, i: (h, b * nq + i)),
            pl.BlockSpec((HEAD_DIM, seq), lambda b, h, i: (n_heads + h, b)),
            pl.BlockSpec((HEAD_DIM, seq), lambda b, h, i: (2 * n_heads + h, b)),
            pl.BlockSpec((None, 1, seq), lambda b, h, i: (h, 0, b)),
        ],
        out_specs=pl.BlockSpec((HEAD_DIM, tq), lambda b, h, i: (h, b * nq + i)),
        out_shape=jax.ShapeDtypeStruct((d_model, tokens), BF16),
        scratch_shapes=[pltpu.VMEM((seq, 2 * HEAD_DIM), BF16)],
        compiler_params=_params(("arbitrary", "arbitrary", "arbitrary")),
        name="fox_attn",
    )(qkv, qkv, qkv, c3)


def _outproj_t_kernel(ot_ref, w_ref, x_ref, y_ref):
    y_ref[...] = x_ref[...] + lax.dot_general(ot_ref[...], w_ref[...], _TN, preferred_element_type=F32)


def _outproj_t(ot, w, x2, *, ts):
    tokens, d_model = x2.shape
    return pl.pallas_call(
        _outproj_t_kernel,
        grid=(tokens // ts,),
        in_specs=[
            pl.BlockSpec((d_model, ts), lambda i: (0, i)),
            pl.BlockSpec((d_model, d_model), lambda i: (0, 0)),
            pl.BlockSpec((ts, d_model), lambda i: (i, 0)),
        ],
        out_specs=pl.BlockSpec((ts, d_model), lambda i: (i, 0)),
        out_shape=jax.ShapeDtypeStruct((tokens, d_model), F32),
        compiler_params=_params(("arbitrary",)),
        name="fox_outproj",
    )(ot, w, x2)


def _mlp_kernel(x_ref, g_ref, w1_ref, w2_ref, y_ref, h_ref, acc_ref):
    f = pl.program_id(1)

    @pl.when(f == 0)
    def _():
        h_ref[...] = _rms_bf16(x_ref[...], g_ref[...])
        acc_ref[...] = jnp.zeros_like(acc_ref)

    a = jnp.dot(h_ref[...], w1_ref[...], preferred_element_type=F32)
    a = jnp.square(jnp.maximum(a, 0.0)).astype(BF16)
    acc_ref[...] += jnp.dot(a, w2_ref[...], preferred_element_type=F32)

    @pl.when(f == pl.num_programs(1) - 1)
    def _():
        y_ref[...] = x_ref[...] + acc_ref[...]


def _mlp(x2, g, w1, w2, *, tm, tf):
    tokens, d_model = x2.shape
    d_ff = w1.shape[1]
    return pl.pallas_call(
        _mlp_kernel,
        grid=(tokens // tm, d_ff // tf),
        in_specs=[
            pl.BlockSpec((tm, d_model), lambda i, f: (i, 0)),
            pl.BlockSpec((1, d_model), lambda i, f: (0, 0)),
            pl.BlockSpec((d_model, tf), lambda i, f: (0, f)),
            pl.BlockSpec((tf, d_model), lambda i, f: (f, 0)),
        ],
        out_specs=pl.BlockSpec((tm, d_model), lambda i, f: (i, 0)),
        out_shape=jax.ShapeDtypeStruct((tokens, d_model), F32),
        scratch_shapes=[pltpu.VMEM((tm, d_model), BF16), pltpu.VMEM((tm, d_model), F32)],
        compiler_params=_params(("arbitrary", "arbitrary")),
        name="mlp",
    )(x2, g, w1, w2)


_TWO_PI = 2.0 * math.pi
_PI2_A = 6.28125
_PI2_B = float((np.array(_TWO_PI - _PI2_A, np.float32).view(np.uint32) & np.uint32(0xFFFFF000)).view(np.float32))
_PI2_C = _TWO_PI - _PI2_A - _PI2_B


def _rope_tables(pos_row, inv_col):
    ang = pos_row.astype(F32) * inv_col
    k = jnp.floor(ang * (1.0 / _TWO_PI) + 0.5)
    r = ((ang - k * _PI2_A) - k * _PI2_B) - k * _PI2_C
    return jnp.cos(r), jnp.sin(r)


def _dil_inproj_kernel(x_ref, pos_ref, inv_ref, g_ref, wqk_ref, wv_ref, gq_ref, gk_ref,
                       q_ref, kt_ref, v_ref):
    h = _rms_bf16(x_ref[...], g_ref[...])
    tl, d_model = x_ref.shape
    n_heads = d_model // HEAD_DIM
    half = ROT_DIM // 2
    cos, sin = _rope_tables(pos_ref[...], inv_ref[...])
    for part, gain_ref, mult in ((0, gq_ref, LOG2E / math.sqrt(HEAD_DIM)), (1, gk_ref, 1.0)):
        rows = slice(part * d_model, (part + 1) * d_model)
        y = lax.dot_general(wqk_ref[rows, :], h, _NT, preferred_element_type=F32)
        y = _head_norm(y.reshape(n_heads, HEAD_DIM, tl), gain_ref[...] * mult)
        x1, x2 = y[:, 0:half], y[:, half:ROT_DIM]
        y = jnp.concatenate([x1 * cos - x2 * sin, x2 * cos + x1 * sin, y[:, ROT_DIM:]], axis=1)
        y = y.reshape(d_model, tl)
        if part == 0:
            q_ref[...] = y.T.astype(BF16)
        else:
            kt_ref[...] = y.astype(BF16)
    v_ref[...] = jnp.dot(h, wv_ref[...], preferred_element_type=F32).astype(BF16)


def _dil_inproj(x3, pos_perm, inv, g, wqk, wv, gq, gk, *, dil, tl):
    batch, sub_len, _ = x3.shape
    d_model = wv.shape[0]
    nl = sub_len // tl
    tokens = batch * dil * sub_len
    flat = lambda b, r, l: (b * dil + r) * nl + l
    const = lambda b, r, l: (0, 0)
    return pl.pallas_call(
        _dil_inproj_kernel,
        grid=(batch, dil, nl),
        in_specs=[
            pl.BlockSpec((None, tl, d_model), lambda b, r, l: (b, l, r)),
            pl.BlockSpec((1, tl), lambda b, r, l: (0, flat(b, r, l))),
            pl.BlockSpec((ROT_DIM // 2, 1), const),
            pl.BlockSpec((1, d_model), const),
            pl.BlockSpec((2 * d_model, d_model), const),
            pl.BlockSpec((d_model, d_model), const),
            pl.BlockSpec((HEAD_DIM, 1), const),
            pl.BlockSpec((HEAD_DIM, 1), const),
        ],
        out_specs=[
            pl.BlockSpec((tl, d_model), lambda b, r, l: (flat(b, r, l), 0)),
            pl.BlockSpec((d_model, tl), lambda b, r, l: (0, flat(b, r, l))),
            pl.BlockSpec((tl, d_model), lambda b, r, l: (flat(b, r, l), 0)),
        ],
        out_shape=[
            jax.ShapeDtypeStruct((tokens, d_model), BF16),
            jax.ShapeDtypeStruct((d_model, tokens), BF16),
            jax.ShapeDtypeStruct((tokens, d_model), BF16),
        ],
        compiler_params=_params(("arbitrary", "arbitrary", "arbitrary")),
        name=f"dil_inproj_d{dil}",
    )(x3, pos_perm, inv, g, wqk, wv, gq, gk)


def _dil_attn_kernel(q_ref, ktp_ref, ktc_ref, vp_ref, vc_ref, o_ref, lse_ref, *, tq, bounded):
    i = pl.program_id(1)
    blk = DIL_BLOCK
    has_prev = i > 0
    d_model = q_ref.shape[1]
    kt_all = jnp.concatenate([ktp_ref[...], ktc_ref[...]], axis=1)
    v_all = jnp.concatenate([vp_ref[...], vc_ref[...]], axis=0)
    r = lax.broadcasted_iota(jnp.int32, (blk, 2 * blk), 0)
    c = lax.broadcasted_iota(jnp.int32, (blk, 2 * blk), 1)
    band = (c >= r) & (c <= r + blk)
    first_mask = band & (has_prev | (c >= blk))
    lane = lax.broadcasted_iota(jnp.int32, (blk, 2 * HEAD_DIM), 1)
    low = lane < HEAD_DIM
    for u in range(tq // blk):
        rows = slice(u * blk, (u + 1) * blk)
        mask = first_mask if u == 0 else band
        stat_tile = jnp.zeros((blk, 2 * HEAD_DIM), F32)
        for pair in range(d_model // (2 * HEAD_DIM)):
            cols = slice(pair * 2 * HEAD_DIM, (pair + 1) * 2 * HEAD_DIM)
            qp = q_ref[rows, cols]
            kt = kt_all[cols, u * blk:(u + 2) * blk]
            vv = v_all[u * blk:(u + 2) * blk, cols]
            outs = []
            for hh, sel in ((0, low), (1, ~low)):
                qh = jnp.where(sel, qp, jnp.zeros_like(qp))
                s = jnp.dot(qh, kt, preferred_element_type=F32)
                s = jnp.where(mask, s, NEG_INF)
                if bounded:
                    p = jnp.exp2(s)
                    stat = jnp.sum(p, axis=1, keepdims=True)
                    outs.append(jnp.dot(p.astype(BF16), vv, preferred_element_type=F32))
                else:
                    m = jnp.max(s, axis=1, keepdims=True)
                    p = jnp.exp2(s - m)
                    den = jnp.sum(p, axis=1, keepdims=True)
                    stat = m + jnp.log2(den)
                    outs.append(jnp.dot(p.astype(BF16), vv, preferred_element_type=F32) * (1.0 / den))
                stat_tile = jnp.where(lane == 2 * pair + hh, stat, stat_tile)
            o_ref[rows, cols] = jnp.where(low, outs[0], outs[1]).astype(BF16)
        lse_ref[rows, :] = stat_tile


def _dil_attn(q, kt, v, *, batch, dil, sub_len, tq, bounded):
    tokens, d_model = q.shape
    nq = sub_len // tq
    per = tq // DIL_BLOCK
    first = lambda s, i: jnp.maximum((s * nq + i) * per - 1, 0)
    return pl.pallas_call(
        functools.partial(_dil_attn_kernel, tq=tq, bounded=bounded),
        grid=(batch * dil, nq),
        in_specs=[
            pl.BlockSpec((tq, d_model), lambda s, i: (s * nq + i, 0)),
            pl.BlockSpec((d_model, DIL_BLOCK), lambda s, i: (0, first(s, i))),
            pl.BlockSpec((d_model, tq), lambda s, i: (0, s * nq + i)),
            pl.BlockSpec((DIL_BLOCK, d_model), lambda s, i: (first(s, i), 0)),
            pl.BlockSpec((tq, d_model), lambda s, i: (s * nq + i, 0)),
        ],
        out_specs=[
            pl.BlockSpec((None, tq, d_model), lambda s, i: (s // dil, i, s % dil)),
            pl.BlockSpec((None, tq, 2 * HEAD_DIM), lambda s, i: (s // dil, i, s % dil)),
        ],
        out_shape=[
            jax.ShapeDtypeStruct((batch, sub_len, dil * d_model), BF16),
            jax.ShapeDtypeStruct((batch, sub_len, dil * 2 * HEAD_DIM), F32),
        ],
        compiler_params=_params(("arbitrary", "arbitrary")),
        name=f"dil_attn_d{dil}" + ("" if bounded else "_exact"),
    )(q, kt, kt, v, v)


def _dil_out_kernel(o0_ref, o1_ref, o2_ref, l0_ref, l1_ref, l2_ref, e_ref, w_ref, x_ref, y_ref, *, bounded):
    o_refs = (o0_ref, o1_ref, o2_ref)
    stats = [l0_ref[...], l1_ref[...], l2_ref[...]]

    def per_head_to_cols(w):
        hi = w.astype(BF16)
        lo = (w - hi.astype(F32)).astype(BF16)
        return jnp.dot(jnp.concatenate([hi, lo], axis=1), e_ref[...], preferred_element_type=F32)

    if bounded:
        n_heads = w_ref.shape[0] // HEAD_DIM
        lane = lax.broadcasted_iota(jnp.int32, stats[0].shape, 1)
        den = jnp.where(lane < n_heads, stats[0] + stats[1] + stats[2], 1.0)
        num = o_refs[0][...].astype(F32) + o_refs[1][...].astype(F32) + o_refs[2][...].astype(F32)
        merged = num * per_head_to_cols(1.0 / den)
    else:
        top = jnp.maximum(jnp.maximum(stats[0], stats[1]), stats[2])
        es = [jnp.exp2(l - top) for l in stats]
        den = es[0] + es[1] + es[2]
        merged = None
        for e, o_ref in zip(es, o_refs):
            term = per_head_to_cols(e / den) * o_ref[...].astype(F32)
            merged = term if merged is None else merged + term
    y_ref[...] = x_ref[...] + jnp.dot(merged.astype(BF16), w_ref[...], preferred_element_type=F32)


def _dil_out(os_, ls_, expand, w, x2, *, ts, bounded):
    tokens, d_model = x2.shape
    row = lambda i: (i, 0)
    const = lambda i: (0, 0)
    return pl.pallas_call(
        functools.partial(_dil_out_kernel, bounded=bounded),
        grid=(tokens // ts,),
        in_specs=[pl.BlockSpec((ts, d_model), row)] * 3 + [pl.BlockSpec((ts, 2 * HEAD_DIM), row)] * 3 + [
            pl.BlockSpec((4 * HEAD_DIM, d_model), const),
            pl.BlockSpec((d_model, d_model), const),
            pl.BlockSpec((ts, d_model), row),
        ],
        out_specs=pl.BlockSpec((ts, d_model), row),
        out_shape=jax.ShapeDtypeStruct((tokens, d_model), F32),
        compiler_params=_params(("arbitrary",)),
        name="dil_outproj" + ("" if bounded else "_exact"),
    )(*os_, *ls_, expand, w, x2)


def _col(v):
    return v.reshape(-1, 1).astype(F32)


def _fox_layer(x2, norm_g, w_in, b_f, q_gain, k_gain, w_out, *, batch, seq, ts, tq, tk, unroll):
    d_model = x2.shape[1]
    n_heads = d_model // HEAD_DIM
    wt = w_in[:, :3 * d_model].T.astype(BF16)
    wf = w_in[:, 3 * d_model:].T.astype(BF16)
    qkv, c = _fox_inproj(x2, norm_g.reshape(1, -1), wt, wf, _col(b_f), _col(q_gain), _col(k_gain),
                         seq=seq, ts=ts)
    c3 = c.reshape(n_heads, 1, -1)

    slack = 1.0 + 2.0 ** -7
    qb = math.sqrt(HEAD_DIM) * jnp.max(jnp.abs(q_gain)) * (LOG2E / math.sqrt(HEAD_DIM)) * slack
    kb = math.sqrt(HEAD_DIM) * jnp.max(jnp.abs(k_gain)) * slack
    per = tq // tk
    ch = c.reshape(n_heads, batch, seq)
    gap = ch[:, :, ::tq, None] - ch[:, :, None, tk - 1::tk]
    below = jnp.arange(seq // tk)[None, :] < (jnp.arange(seq // tq) * per)[:, None]
    jlo = jnp.sum((gap < -FOX_SKIP_LOG2) & below, axis=-1, dtype=jnp.int32)
    jlo = jlo.transpose(1, 0, 2).reshape(-1)

    ot = lax.cond(
        2.0 * qb * kb <= FOX_MAX_LOOSENESS_LOG2,
        lambda: _fox_attn_fast(qkv, c3, jlo, kb.reshape(1).astype(F32), batch=batch, seq=seq,
                               tq=tq, tk=tk, unroll=unroll),
        lambda: _fox_attn_exact(qkv, c3, batch=batch, seq=seq, tq=tq),
    )
    return _outproj_t(ot, w_out.astype(BF16), x2, ts=ts)


def _dil_layer(x2, positions, norm_g, w_in, q_gain, k_gain, w_out, *, batch, seq, tl, tq, ts):
    d_model = x2.shape[1]
    n_heads = d_model // HEAD_DIM
    inv = (ROPE_THETA ** (-np.arange(ROT_DIM // 2, dtype=np.float64) * 2.0 / ROT_DIM)).astype(np.float32)
    inv = jnp.asarray(inv).reshape(-1, 1)
    qkvs = []
    for g, (window, dil) in enumerate(DIL_GROUPS):
        assert window // dil == DIL_BLOCK
        sub_len = seq // dil
        base = g * 3 * d_model
        wqk = w_in[:, base:base + 2 * d_model].T.astype(BF16)
        wv = w_in[:, base + 2 * d_model:base + 3 * d_model].astype(BF16)
        pos_perm = positions.reshape(batch, sub_len, dil).transpose(0, 2, 1).reshape(1, -1)
        x3 = x2.reshape(batch, sub_len, dil * d_model)
        qkvs.append(_dil_inproj(x3, pos_perm, inv, norm_g.reshape(1, -1), wqk, wv,
                                _col(q_gain[g]), _col(k_gain[g]), dil=dil, tl=min(tl, sub_len)))
    head_of_col = np.arange(d_model) // HEAD_DIM
    expand = (np.arange(2 * HEAD_DIM)[:, None] == head_of_col[None, :]).astype(np.float32)
    expand = jnp.asarray(np.concatenate([expand, expand], axis=0), BF16)
    assert n_heads <= 2 * HEAD_DIM
    w_out = w_out.astype(BF16)

    def attend_and_merge(bounded):
        outs, stats = [], []
        for (q, kt, v), (_, dil) in zip(qkvs, DIL_GROUPS):
            sub_len = seq // dil
            o, stat = _dil_attn(q, kt, v, batch=batch, dil=dil, sub_len=sub_len, tq=min(tq, sub_len),
                                bounded=bounded)
            outs.append(o.reshape(batch * seq, d_model))
            stats.append(stat.reshape(batch * seq, 2 * HEAD_DIM))
        return _dil_out(outs, stats, expand, w_out, x2, ts=ts, bounded=bounded)

    bound = (math.sqrt(HEAD_DIM) * LOG2E * (1.0 + 2.0 ** -6)
             * jnp.max(jnp.max(jnp.abs(q_gain), axis=1) * jnp.max(jnp.abs(k_gain), axis=1)))
    return lax.cond(bound <= DIL_MAX_SCORE_LOG2, functools.partial(attend_and_merge, True),
                    functools.partial(attend_and_merge, False))


def _forward(x, positions, fox_norm, fox_w_in, fox_b_f, fox_q_gain, fox_k_gain, fox_w_out,
             dil_norm, dil_w_in, dil_q_gain, dil_k_gain, dil_w_out, mlp_norm, mlp_w1, mlp_w2,
             *, ts, tq_fox, tk_fox, unroll_fox, tl, tq_dil, tm, tf):
    batch, seq, d_model = x.shape
    x2 = x.reshape(batch * seq, d_model)
    depth = mlp_norm.shape[0]
    for layer in range(depth):
        j = layer // 2
        if layer % 2 == 0:
            x2 = _fox_layer(x2, fox_norm[j], fox_w_in[j], fox_b_f[j], fox_q_gain[j], fox_k_gain[j],
                            fox_w_out[j], batch=batch, seq=seq, ts=ts, tq=tq_fox, tk=tk_fox,
                            unroll=unroll_fox)
        else:
            x2 = _dil_layer(x2, positions, dil_norm[j], dil_w_in[j], dil_q_gain[j], dil_k_gain[j],
                            dil_w_out[j], batch=batch, seq=seq, tl=tl, tq=tq_dil, ts=ts)
        x2 = _mlp(x2, mlp_norm[layer].reshape(1, -1), mlp_w1[layer].astype(BF16),
                  mlp_w2[layer].astype(BF16), tm=tm, tf=tf)
    return x2.reshape(batch, seq, d_model)


def kernel(x, positions, fox_norm, fox_w_in, fox_b_f, fox_q_gain, fox_k_gain, fox_w_out, dil_norm, dil_w_in, dil_q_gain, dil_k_gain, dil_w_out, mlp_norm, mlp_w1, mlp_w2):
    return _forward(x, positions, fox_norm, fox_w_in, fox_b_f, fox_q_gain, fox_k_gain, fox_w_out,
                    dil_norm, dil_w_in, dil_q_gain, dil_k_gain, dil_w_out, mlp_norm, mlp_w1, mlp_w2,
                    ts=512, tq_fox=512, tk_fox=256, unroll_fox=6, tl=512, tq_dil=512, tm=1024, tf=2048)
```

```python
import functools
import math

import jax
import jax.numpy as jnp
import numpy as np
from jax import lax
from jax.experimental import pallas as pl
from jax.experimental.pallas import tpu as pltpu

F32, BF16 = jnp.float32, jnp.bfloat16
HEAD_DIM = 64
ROT_DIM = 16
ROPE_THETA = 500000.0
DIL_GROUPS = ((128, 1), (512, 4), (2048, 16))
DIL_BLOCK = 128
EPS = 1e-6
NEG_INF = -1e30
LOG2E = 1.4426950408889634
LANES = 128
BIAS_ROWS = 3
FOX_SKIP_LOG2 = 160.0
FOX_MAX_LOOSENESS_LOG2 = 100.0
DIL_MAX_SCORE_LOG2 = 50.0
VMEM_LIMIT = 56 * 1024 * 1024

_NT = (((1,), (1,)), ((), ()))
_TN = (((0,), (0,)), ((), ()))


def _params(sem):
    return pltpu.CompilerParams(dimension_semantics=sem, vmem_limit_bytes=VMEM_LIMIT)


def _rms(x, g):
    return x * lax.rsqrt(jnp.mean(x * x, axis=-1, keepdims=True) + EPS) * g


def _head_norm(y, gain):
    inv = lax.rsqrt(jnp.mean(y * y, axis=1, keepdims=True) + EPS)
    return y * inv * gain[None]


def _split3(x):
    hi = x.astype(BF16).astype(F32)
    rem = x - hi
    mid = rem.astype(BF16).astype(F32)
    return hi, mid, rem - mid


def _fox_inproj_kernel(x_ref, g_ref, wt_ref, wf_ref, bf_ref, gq_ref, gk_ref,
                       qkv_ref, c_ref, carry_ref, *, tiles_per_batch):
    i = pl.program_id(0)

    @pl.when(i % tiles_per_batch == 0)
    def _():
        carry_ref[...] = jnp.zeros_like(carry_ref)

    h = _rms(x_ref[...], g_ref[...]).astype(BF16)
    tt, d_model = x_ref.shape
    n_heads = d_model // HEAD_DIM

    f = lax.dot_general(wf_ref[...], h, _NT, preferred_element_type=F32) + bf_ref[...]
    lf = (jnp.minimum(f, 0.0) - jnp.log1p(jnp.exp(-jnp.abs(f)))) * LOG2E
    lane = lax.broadcasted_iota(jnp.int32, lf.shape, 1)
    shift = 1
    while shift < tt:
        lf = lf + jnp.where(lane >= shift, pltpu.roll(lf, shift, 1), 0.0)
        shift *= 2
    c = lf + carry_ref[:, 0:1]
    c_ref[...] = c
    carry_ref[...] = jnp.broadcast_to(c[:, tt - 1:tt], carry_ref.shape)

    for part, gain_ref, mult in ((0, gq_ref, LOG2E / math.sqrt(HEAD_DIM)), (1, gk_ref, 1.0)):
        rows = slice(part * d_model, (part + 1) * d_model)
        y = lax.dot_general(wt_ref[rows, :], h, _NT, preferred_element_type=F32)
        y = _head_norm(y.reshape(n_heads, HEAD_DIM, tt), gain_ref[...] * mult)
        qkv_ref[rows, :] = y.reshape(d_model, tt).astype(BF16)
    rows = slice(2 * d_model, 3 * d_model)
    qkv_ref[rows, :] = lax.dot_general(wt_ref[rows, :], h, _NT, preferred_element_type=F32).astype(BF16)


def _fox_inproj(x2, g, wt, wf, bf, gq, gk, *, seq, tt):
    tokens, d_model = x2.shape
    n_heads = d_model // HEAD_DIM
    nt = tokens // tt
    const = lambda i: (0, 0)
    return pl.pallas_call(
        functools.partial(_fox_inproj_kernel, tiles_per_batch=seq // tt),
        grid=(nt,),
        in_specs=[
            pl.BlockSpec((tt, d_model), lambda i: (i, 0)),
            pl.BlockSpec((1, d_model), const),
            pl.BlockSpec((3 * d_model, d_model), const),
            pl.BlockSpec((n_heads, d_model), const),
            pl.BlockSpec((n_heads, 1), const),
            pl.BlockSpec((HEAD_DIM, 1), const),
            pl.BlockSpec((HEAD_DIM, 1), const),
        ],
        out_specs=[
            pl.BlockSpec((None, 3 * d_model, tt), lambda i: (i, 0, 0)),
            pl.BlockSpec((None, n_heads, tt), lambda i: (i, 0, 0)),
        ],
        out_shape=[
            jax.ShapeDtypeStruct((nt, 3 * d_model, tt), BF16),
            jax.ShapeDtypeStruct((nt, n_heads, tt), F32),
        ],
        scratch_shapes=[pltpu.VMEM((n_heads, LANES), F32)],
        compiler_params=_params(("arbitrary",)),
        name="fox_inproj",
    )(x2, g, wt, wf, bf, gq, gk)


def _build_keys(k_ref, c_ref, kp_ref, extra_rows):
    tiles, _, tt = k_ref.shape
    pad = jnp.zeros((HEAD_DIM - BIAS_ROWS - extra_rows, tt), F32)
    ones = [jnp.ones((extra_rows, tt), F32)] if extra_rows else []
    for n in range(tiles):
        blk = jnp.concatenate([k_ref[n].astype(F32), *_split3(-c_ref[n]), *ones, pad], axis=0)
        kp_ref[n * tt:(n + 1) * tt, :] = blk.T.astype(BF16)


def _fox_exact_kernel(q_ref, k_ref, v_ref, c_ref, o_ref, kp_ref):
    i = pl.program_id(2)
    tt = q_ref.shape[1]

    @pl.when(i == 0)
    def _():
        _build_keys(k_ref, c_ref, kp_ref, 0)

    ones_rows = (lax.broadcasted_iota(jnp.int32, (HEAD_DIM, tt), 0) < BIAS_ROWS).astype(BF16)
    qp = jnp.concatenate([q_ref[...], ones_rows], axis=0)

    def step(j, carry, masked):
        m, l, acc = carry
        off = pl.multiple_of(j * tt, tt)
        s = jnp.dot(kp_ref[pl.ds(off, tt), :], qp, preferred_element_type=F32)
        if masked:
            key = lax.broadcasted_iota(jnp.int32, s.shape, 0)
            qry = lax.broadcasted_iota(jnp.int32, s.shape, 1)
            s = jnp.where(key <= qry, s, NEG_INF)
        m_new = jnp.maximum(m, jnp.max(s, axis=0, keepdims=True))
        alpha = jnp.exp2(m - m_new)
        p = jnp.exp2(s - m_new)
        l = alpha * l + jnp.sum(p, axis=0, keepdims=True)
        pv = jnp.dot(v_ref[j], p.astype(BF16), preferred_element_type=F32)
        return m_new, l, alpha * acc + pv

    init = (jnp.full((1, tt), NEG_INF, F32), jnp.zeros((1, tt), F32), jnp.zeros((HEAD_DIM, tt), F32))
    carry = lax.fori_loop(0, i, lambda j, c: step(j, c, False), init)
    m, l, acc = step(i, carry, True)
    o_ref[...] = (acc / l).astype(BF16)


def _fox_fast_kernel(jlo_ref, kb_ref, q_ref, k_ref, v_ref, c_ref, o_ref, kp_ref, *s_refs, tk, unroll):
    bh = pl.program_id(0) * pl.num_programs(1) + pl.program_id(1)
    nq, _, tq = q_ref.shape
    per, nkv = tq // tk, nq * (tq // tk)
    kb = kb_ref[0]
    pad_rows = HEAD_DIM - 2 * BIAS_ROWS

    _build_keys(k_ref, c_ref, kp_ref, BIAS_ROWS)
    lane = lax.broadcasted_iota(jnp.int32, (tk, 2 * HEAD_DIM), 1)
    kp_ref[nq * tq:nq * tq + tk, :] = jnp.where(lane == HEAD_DIM, NEG_INF, 0.0).astype(BF16)

    ones_row = (lax.broadcasted_iota(jnp.int32, (16, tk), 0) == 0).astype(BF16)
    key = lax.broadcasted_iota(jnp.int32, (tk, tq), 0)
    qry = lax.broadcasted_iota(jnp.int32, (tk, tq), 1)

    def q_block(i, _):
        qt = q_ref[i].astype(F32)
        qn = jnp.sqrt(jnp.sum(qt * qt, axis=0, keepdims=True))
        neg_m = c_ref[i] - qn * kb
        qp = jnp.concatenate([qt, jnp.ones((BIAS_ROWS, tq), F32), *_split3(neg_m),
                              jnp.zeros((pad_rows, tq), F32)], axis=0).astype(BF16)
        jlo = jlo_ref[bh * nq + i]
        n_off = i * per - jlo
        n_it = (n_off + unroll - 1) // unroll

        def qk(blk):
            off = pl.multiple_of(blk * tk, tk)
            return jnp.dot(kp_ref[pl.ds(off, tk), :], qp, preferred_element_type=F32)

        def pv(blk, s, acc):
            off = pl.multiple_of(lax.rem(blk, per) * tk, tk)
            p = jnp.exp2(s).astype(BF16)
            vv = jnp.concatenate([v_ref[lax.div(blk, per), :, pl.ds(off, tk)], ones_row], axis=0)
            return acc + jnp.dot(vv, p, preferred_element_type=F32)

        def off_tile(g):
            return jnp.where(g < n_off, jlo + g, nkv), jnp.where(g < n_off, jlo + g, 0)

        def consumed_v(g):
            return jnp.where(g < per, i * per + g, off_tile(g - per)[1])

        for t in range(per):
            s = qk(i * per + t)
            s_refs[t][...] = jnp.where(key + t * tk <= qry, s, NEG_INF)

        def body(u, acc):
            pend = [s_refs[t][...] for t in range(per)]
            for t in range(unroll):
                g = u * unroll + t
                pend.append(qk(off_tile(g)[0]))
                acc = pv(consumed_v(g), pend.pop(0), acc)
            for t in range(per):
                s_refs[t][...] = pend[t]
            return acc

        acc = lax.fori_loop(0, n_it, body, jnp.zeros((HEAD_DIM + 16, tq), F32))
        for t in range(per):
            acc = pv(consumed_v(n_it * unroll + t), s_refs[t][...], acc)
        o_ref[i] = (acc[0:HEAD_DIM] / acc[HEAD_DIM:HEAD_DIM + 1]).astype(BF16)
        return 0

    lax.fori_loop(0, nq, q_block, 0)


def _fox_attn_fast(qkv, c4, jlo, kb, *, batch, seq, tk, unroll):
    nt, rows, tt = qkv.shape
    d_model = rows // 3
    n_heads = d_model // HEAD_DIM
    tpb = seq // tt
    grid_spec = pltpu.PrefetchScalarGridSpec(
        num_scalar_prefetch=1,
        grid=(batch, n_heads),
        in_specs=[
            pl.BlockSpec(memory_space=pltpu.SMEM),
            pl.BlockSpec((tpb, HEAD_DIM, tt), lambda b, h, jlo: (b, h, 0)),
            pl.BlockSpec((tpb, HEAD_DIM, tt), lambda b, h, jlo: (b, n_heads + h, 0)),
            pl.BlockSpec((tpb, HEAD_DIM, tt), lambda b, h, jlo: (b, 2 * n_heads + h, 0)),
            pl.BlockSpec((tpb, None, 1, tt), lambda b, h, jlo: (b, h, 0, 0)),
        ],
        out_specs=pl.BlockSpec((tpb, HEAD_DIM, tt), lambda b, h, jlo: (b, h, 0)),
        scratch_shapes=[pltpu.VMEM((seq + tk, 2 * HEAD_DIM), BF16)] + [pltpu.VMEM((tk, tt), F32)] * (tt // tk),
    )
    return pl.pallas_call(
        functools.partial(_fox_fast_kernel, tk=tk, unroll=unroll),
        grid_spec=grid_spec,
        out_shape=jax.ShapeDtypeStruct((nt, d_model, tt), BF16),
        compiler_params=_params(("arbitrary", "arbitrary")),
        name="fox_attn_fast",
    )(jlo, kb, qkv, qkv, qkv, c4)


def _fox_attn_exact(qkv, c4, *, batch, seq):
    nt, rows, tt = qkv.shape
    d_model = rows // 3
    n_heads = d_model // HEAD_DIM
    tpb = seq // tt
    return pl.pallas_call(
        _fox_exact_kernel,
        grid=(batch, n_heads, tpb),
        in_specs=[
            pl.BlockSpec((None, HEAD_DIM, tt), lambda b, h, i: (b * tpb + i, h, 0)),
            pl.BlockSpec((tpb, HEAD_DIM, tt), lambda b, h, i: (b, n_heads + h, 0)),
            pl.BlockSpec((tpb, HEAD_DIM, tt), lambda b, h, i: (b, 2 * n_heads + h, 0)),
            pl.BlockSpec((tpb, None, 1, tt), lambda b, h, i: (b, h, 0, 0)),
        ],
        out_specs=pl.BlockSpec((None, HEAD_DIM, tt), lambda b, h, i: (b * tpb + i, h, 0)),
        out_shape=jax.ShapeDtypeStruct((nt, d_model, tt), BF16),
        scratch_shapes=[pltpu.VMEM((seq, 2 * HEAD_DIM), BF16)],
        compiler_params=_params(("arbitrary", "arbitrary", "arbitrary")),
        name="fox_attn_exact",
    )(qkv, qkv, qkv, c4)


def _outproj_t_kernel(ot_ref, w_ref, x_ref, y_ref):
    y_ref[...] = x_ref[...] + lax.dot_general(ot_ref[...], w_ref[...], _TN, preferred_element_type=F32)


def _outproj_t(ot, w, x2):
    nt, d_model, tt = ot.shape
    return pl.pallas_call(
        _outproj_t_kernel,
        grid=(nt,),
        in_specs=[
            pl.BlockSpec((None, d_model, tt), lambda i: (i, 0, 0)),
            pl.BlockSpec((d_model, d_model), lambda i: (0, 0)),
            pl.BlockSpec((tt, d_model), lambda i: (i, 0)),
        ],
        out_specs=pl.BlockSpec((tt, d_model), lambda i: (i, 0)),
        out_shape=jax.ShapeDtypeStruct(x2.shape, F32),
        compiler_params=_params(("arbitrary",)),
        name="fox_outproj",
    )(ot, w, x2)


def _mlp_kernel(x_ref, g_ref, w1_ref, w2_ref, y_ref, h_ref, acc_ref):
    f = pl.program_id(1)

    @pl.when(f == 0)
    def _():
        h_ref[...] = _rms(x_ref[...], g_ref[...]).astype(BF16)
        acc_ref[...] = jnp.zeros_like(acc_ref)

    a = jnp.dot(h_ref[...], w1_ref[...], preferred_element_type=F32)
    a = jnp.square(jnp.maximum(a, 0.0)).astype(BF16)
    acc_ref[...] += jnp.dot(a, w2_ref[...], preferred_element_type=F32)

    @pl.when(f == pl.num_programs(1) - 1)
    def _():
        y_ref[...] = x_ref[...] + acc_ref[...]


def _mlp(x2, g, w1, w2, *, tm, tf):
    tokens, d_model = x2.shape
    d_ff = w1.shape[1]
    return pl.pallas_call(
        _mlp_kernel,
        grid=(tokens // tm, d_ff // tf),
        in_specs=[
            pl.BlockSpec((tm, d_model), lambda i, f: (i, 0)),
            pl.BlockSpec((1, d_model), lambda i, f: (0, 0)),
            pl.BlockSpec((d_model, tf), lambda i, f: (0, f)),
            pl.BlockSpec((tf, d_model), lambda i, f: (f, 0)),
        ],
        out_specs=pl.BlockSpec((tm, d_model), lambda i, f: (i, 0)),
        out_shape=jax.ShapeDtypeStruct((tokens, d_model), F32),
        scratch_shapes=[pltpu.VMEM((tm, d_model), BF16), pltpu.VMEM((tm, d_model), F32)],
        compiler_params=_params(("arbitrary", "arbitrary")),
        name="mlp",
    )(x2, g, w1, w2)


def _dil_prep_kernel(x_ref, g_ref, *refs, dils):
    out_refs, slab_ref = refs[:-1], refs[-1]
    h = _rms(x_ref[...], g_ref[...])
    tt, d_model = h.shape
    n_slabs = d_model // LANES
    if any(d > 1 for d in dils):
        for j in range(n_slabs):
            slab_ref[j] = h[:, j * LANES:(j + 1) * LANES]
    for dil, o_ref in zip(dils, out_refs):
        if dil == 1:
            o_ref[...] = h.astype(BF16)
            continue
        n = tt // dil
        for r in range(dil):
            rows = [slab_ref[j, pl.ds(r, n, stride=dil), :] for j in range(n_slabs)]
            o_ref[r] = jnp.concatenate(rows, axis=1).astype(BF16)


def _dil_prep(x2, g, *, batch, seq, tt, dils):
    tokens, d_model = x2.shape
    tpb = seq // tt
    out_specs, out_shape = [], []
    for dil in dils:
        if dil == 1:
            out_specs.append(pl.BlockSpec((tt, d_model), lambda i: (i, 0)))
            out_shape.append(jax.ShapeDtypeStruct((tokens, d_model), BF16))
        else:
            out_specs.append(pl.BlockSpec((None, dil, tt // dil, d_model), lambda i: (i // tpb, 0, i % tpb, 0)))
            out_shape.append(jax.ShapeDtypeStruct((batch, dil, seq // dil, d_model), BF16))
    return pl.pallas_call(
        functools.partial(_dil_prep_kernel, dils=dils),
        grid=(tokens // tt,),
        in_specs=[pl.BlockSpec((tt, d_model), lambda i: (i, 0)), pl.BlockSpec((1, d_model), lambda i: (0, 0))],
        out_specs=out_specs,
        out_shape=out_shape,
        scratch_shapes=[pltpu.VMEM((d_model // LANES, tt, LANES), F32)],
        compiler_params=_params(("arbitrary",)),
        name="dil_prep",
    )(x2, g)


_TWO_PI = 2.0 * math.pi
_PI2_A = 6.28125
_PI2_B = float((np.array(_TWO_PI - _PI2_A, np.float32).view(np.uint32) & np.uint32(0xFFFFF000)).view(np.float32))
_PI2_C = _TWO_PI - _PI2_A - _PI2_B


def _rope_tables(pos_row, inv_col):
    ang = pos_row.astype(F32) * inv_col
    k = jnp.floor(ang * (1.0 / _TWO_PI) + 0.5)
    r = ((ang - k * _PI2_A) - k * _PI2_B) - k * _PI2_C
    return jnp.cos(r), jnp.sin(r)


def _dil_inproj_kernel(h_ref, pos_ref, inv_ref, wqk_ref, wv_ref, gq_ref, gk_ref, q_ref, kt_ref, v_ref):
    h = h_ref[...]
    tt, d_model = h.shape
    n_heads = d_model // HEAD_DIM
    half = ROT_DIM // 2
    cos, sin = _rope_tables(pos_ref[...], inv_ref[...])
    for part, gain_ref, mult in ((0, gq_ref, LOG2E / math.sqrt(HEAD_DIM)), (1, gk_ref, 1.0)):
        rows = slice(part * d_model, (part + 1) * d_model)
        y = lax.dot_general(wqk_ref[rows, :], h, _NT, preferred_element_type=F32)
        y = _head_norm(y.reshape(n_heads, HEAD_DIM, tt), gain_ref[...] * mult)
        x1, x2 = y[:, 0:half], y[:, half:ROT_DIM]
        y = jnp.concatenate([x1 * cos - x2 * sin, x2 * cos + x1 * sin, y[:, ROT_DIM:]], axis=1)
        y = y.reshape(d_model, tt)
        if part == 0:
            q_ref[...] = y.T.astype(BF16)
        else:
            kt_ref[...] = y.astype(BF16)
    v_ref[...] = jnp.dot(h, wv_ref[...], preferred_element_type=F32).astype(BF16)


def _dil_inproj(h2, pos_perm, inv, wqk, wv, gq, gk, *, tt, name):
    tokens, d_model = h2.shape
    nt = tokens // tt
    const = lambda i: (0, 0)
    return pl.pallas_call(
        _dil_inproj_kernel,
        grid=(nt,),
        in_specs=[
            pl.BlockSpec((tt, d_model), lambda i: (i, 0)),
            pl.BlockSpec((1, tt), lambda i: (0, i)),
            pl.BlockSpec((ROT_DIM // 2, 1), const),
            pl.BlockSpec((2 * d_model, d_model), const),
            pl.BlockSpec((d_model, d_model), const),
            pl.BlockSpec((HEAD_DIM, 1), const),
            pl.BlockSpec((HEAD_DIM, 1), const),
        ],
        out_specs=[
            pl.BlockSpec((tt, d_model), lambda i: (i, 0)),
            pl.BlockSpec((None, d_model, tt), lambda i: (i, 0, 0)),
            pl.BlockSpec((tt, d_model), lambda i: (i, 0)),
        ],
        out_shape=[
            jax.ShapeDtypeStruct((tokens, d_model), BF16),
            jax.ShapeDtypeStruct((nt, d_model, tt), BF16),
            jax.ShapeDtypeStruct((tokens, d_model), BF16),
        ],
        compiler_params=_params(("arbitrary",)),
        name=name,
    )(h2, pos_perm, inv, wqk, wv, gq, gk)


def _dil_attn_kernel(q_ref, ktp_ref, ktc_ref, vp_ref, vc_ref, o_ref, stat_ref, *, bounded):
    i = pl.program_id(1)
    blk = DIL_BLOCK
    has_prev = i > 0
    tq, d_model = q_ref.shape
    kt_all = jnp.concatenate([ktp_ref[...], ktc_ref[...]], axis=1)
    v_all = jnp.concatenate([vp_ref[...], vc_ref[...]], axis=0)
    r = lax.broadcasted_iota(jnp.int32, (blk, 2 * blk), 0)
    c = lax.broadcasted_iota(jnp.int32, (blk, 2 * blk), 1)
    band = (c >= r) & (c <= r + blk)
    first_mask = band & (has_prev | (c >= blk))
    lane = lax.broadcasted_iota(jnp.int32, (blk, 2 * HEAD_DIM), 1)
    low = lane < HEAD_DIM
    for u in range(tq // blk):
        rows = slice(u * blk, (u + 1) * blk)
        mask = first_mask if u == 0 else band
        stat_tile = jnp.zeros((blk, 2 * HEAD_DIM), F32)
        for pair in range(d_model // (2 * HEAD_DIM)):
            cols = slice(pair * 2 * HEAD_DIM, (pair + 1) * 2 * HEAD_DIM)
            qp = q_ref[rows, cols]
            kt = kt_all[cols, u * blk:(u + 2) * blk]
            vv = v_all[u * blk:(u + 2) * blk, cols]
            outs = []
            for hh, sel in ((0, low), (1, ~low)):
                qh = jnp.where(sel, qp, jnp.zeros_like(qp))
                s = jnp.dot(qh, kt, preferred_element_type=F32)
                s = jnp.where(mask, s, NEG_INF)
                if bounded:
                    p = jnp.exp2(s)
                    stat = jnp.sum(p, axis=1, keepdims=True)
                    outs.append(jnp.dot(p.astype(BF16), vv, preferred_element_type=F32))
                else:
                    m = jnp.max(s, axis=1, keepdims=True)
                    p = jnp.exp2(s - m)
                    den = jnp.sum(p, axis=1, keepdims=True)
                    stat = m + jnp.log2(den)
                    outs.append(jnp.dot(p.astype(BF16), vv, preferred_element_type=F32) * (1.0 / den))
                stat_tile = jnp.where(lane == 2 * pair + hh, stat, stat_tile)
            o_ref[rows, cols] = jnp.where(low, outs[0], outs[1]).astype(BF16)
        stat_ref[rows, :] = stat_tile


def _dil_attn(q, kt, v, *, n_seqs, tt, bounded, name):
    tokens, d_model = q.shape
    nq = tokens // tt // n_seqs
    per = tt // DIL_BLOCK
    tile = lambda s, i: s * nq + i
    prev_tile = lambda s, i: jnp.maximum(tile(s, i) - 1, 0)
    return pl.pallas_call(
        functools.partial(_dil_attn_kernel, bounded=bounded),
        grid=(n_seqs, nq),
        in_specs=[
            pl.BlockSpec((tt, d_model), lambda s, i: (tile(s, i), 0)),
            pl.BlockSpec((None, d_model, DIL_BLOCK), lambda s, i: (prev_tile(s, i), 0, per - 1)),
            pl.BlockSpec((None, d_model, tt), lambda s, i: (tile(s, i), 0, 0)),
            pl.BlockSpec((DIL_BLOCK, d_model), lambda s, i: ((prev_tile(s, i) + 1) * per - 1, 0)),
            pl.BlockSpec((tt, d_model), lambda s, i: (tile(s, i), 0)),
        ],
        out_specs=[
            pl.BlockSpec((tt, d_model), lambda s, i: (tile(s, i), 0)),
            pl.BlockSpec((tt, 2 * HEAD_DIM), lambda s, i: (tile(s, i), 0)),
        ],
        out_shape=[
            jax.ShapeDtypeStruct((tokens, d_model), BF16),
            jax.ShapeDtypeStruct((tokens, 2 * HEAD_DIM), F32),
        ],
        compiler_params=_params(("arbitrary", "arbitrary")),
        name=name,
    )(q, kt, kt, v, v)


def _dil_out_kernel(*refs, dils, bounded):
    n = len(dils)
    o_refs, stat_refs = refs[:n], refs[n:2 * n]
    e_ref, w_ref, x_ref, y_ref, o_slab, stat_slab = refs[2 * n:]
    tt, d_model = x_ref.shape
    n_slabs = d_model // LANES

    def natural(ref, dil, slab):
        if dil == 1:
            return ref[...].astype(F32)
        rows, width = tt // dil, ref.shape[-1]
        for r in range(dil):
            val = ref[r].astype(F32)
            for j in range(width // LANES):
                slab[j, pl.ds(r, rows, stride=dil), :] = val[:, j * LANES:(j + 1) * LANES]
        return jnp.concatenate([slab[j] for j in range(width // LANES)], axis=1)

    stats = [natural(ref, dil, stat_slab) for ref, dil in zip(stat_refs, dils)]
    outs = lambda g: natural(o_refs[g], dils[g], o_slab)

    def per_head_to_cols(w):
        hi = w.astype(BF16)
        lo = (w - hi.astype(F32)).astype(BF16)
        return jnp.dot(jnp.concatenate([hi, lo], axis=1), e_ref[...], preferred_element_type=F32)

    if bounded:
        lane = lax.broadcasted_iota(jnp.int32, stats[0].shape, 1)
        den = jnp.where(lane < d_model // HEAD_DIM, sum(stats[1:], stats[0]), 1.0)
        num = outs(0)
        for g in range(1, n):
            num = num + outs(g)
        merged = num * per_head_to_cols(1.0 / den)
    else:
        top = functools.reduce(jnp.maximum, stats)
        es = [jnp.exp2(l - top) for l in stats]
        den = sum(es[1:], es[0])
        merged = None
        for g in range(n):
            term = per_head_to_cols(es[g] / den) * outs(g)
            merged = term if merged is None else merged + term
    y_ref[...] = x_ref[...] + jnp.dot(merged.astype(BF16), w_ref[...], preferred_element_type=F32)


def _dil_out(outs, stats, expand, w, x2, *, batch, seq, tt, dils, bounded):
    tokens, d_model = x2.shape
    tpb = seq // tt

    def view_and_spec(a, dil):
        width = a.shape[-1]
        if dil == 1:
            return a, pl.BlockSpec((tt, width), lambda i: (i, 0))
        spec = pl.BlockSpec((None, dil, tt // dil, width), lambda i: (i // tpb, 0, i % tpb, 0))
        return a.reshape(batch, dil, seq // dil, width), spec

    args, specs = [], []
    for group in (outs, stats):
        for a, dil in zip(group, dils):
            a, spec = view_and_spec(a, dil)
            args.append(a)
            specs.append(spec)
    row = lambda i: (i, 0)
    const = lambda i: (0, 0)
    return pl.pallas_call(
        functools.partial(_dil_out_kernel, dils=dils, bounded=bounded),
        grid=(tokens // tt,),
        in_specs=specs + [
            pl.BlockSpec((4 * HEAD_DIM, d_model), const),
            pl.BlockSpec((d_model, d_model), const),
            pl.BlockSpec((tt, d_model), row),
        ],
        out_specs=pl.BlockSpec((tt, d_model), row),
        out_shape=jax.ShapeDtypeStruct((tokens, d_model), F32),
        scratch_shapes=[pltpu.VMEM((d_model // LANES, tt, LANES), F32), pltpu.VMEM((1, tt, LANES), F32)],
        compiler_params=_params(("arbitrary",)),
        name="dil_outproj" + ("" if bounded else "_exact"),
    )(*args, expand, w, x2)


def _col(v):
    return v.reshape(-1, 1).astype(F32)


def _fox_layer(x2, norm_g, w_in, b_f, q_gain, k_gain, w_out, *, batch, seq, tt, tk, unroll):
    d_model = x2.shape[1]
    n_heads = d_model // HEAD_DIM
    tpb = seq // tt
    wt = w_in[:, :3 * d_model].T.astype(BF16)
    wf = w_in[:, 3 * d_model:].T.astype(BF16)
    qkv, c = _fox_inproj(x2, norm_g.reshape(1, -1), wt, wf, _col(b_f), _col(q_gain), _col(k_gain),
                         seq=seq, tt=tt)
    c4 = c.reshape(batch * tpb, n_heads, 1, tt)

    slack = 1.0 + 2.0 ** -7
    qb = math.sqrt(HEAD_DIM) * jnp.max(jnp.abs(q_gain)) * (LOG2E / math.sqrt(HEAD_DIM)) * slack
    kb = math.sqrt(HEAD_DIM) * jnp.max(jnp.abs(k_gain)) * slack
    per = tt // tk
    ch = c.reshape(batch, tpb, n_heads, tt).transpose(0, 2, 1, 3).reshape(batch, n_heads, seq)
    gap = ch[:, :, ::tt, None] - ch[:, :, None, tk - 1::tk]
    below = jnp.arange(seq // tk)[None, :] < (jnp.arange(tpb) * per)[:, None]
    jlo = jnp.sum((gap < -FOX_SKIP_LOG2) & below, axis=-1, dtype=jnp.int32).reshape(-1)

    ot = lax.cond(
        2.0 * qb * kb <= FOX_MAX_LOOSENESS_LOG2,
        lambda: _fox_attn_fast(qkv, c4, jlo, kb.reshape(1).astype(F32), batch=batch, seq=seq,
                               tk=tk, unroll=unroll),
        lambda: _fox_attn_exact(qkv, c4, batch=batch, seq=seq),
    )
    return _outproj_t(ot, w_out.astype(BF16), x2)


def _dil_layer(x2, positions, norm_g, w_in, q_gain, k_gain, w_out, *, batch, seq, tt):
    d_model = x2.shape[1]
    n_heads = d_model // HEAD_DIM
    dils = tuple(dil for _, dil in DIL_GROUPS)
    assert all(window // dil == DIL_BLOCK for window, dil in DIL_GROUPS)
    inv = (ROPE_THETA ** (-np.arange(ROT_DIM // 2, dtype=np.float64) * 2.0 / ROT_DIM)).astype(np.float32)
    inv = jnp.asarray(inv).reshape(-1, 1)
    hs = _dil_prep(x2, norm_g.reshape(1, -1), batch=batch, seq=seq, tt=tt, dils=dils)
    qkvs = []
    for g, dil in enumerate(dils):
        base = g * 3 * d_model
        wqk = w_in[:, base:base + 2 * d_model].T.astype(BF16)
        wv = w_in[:, base + 2 * d_model:base + 3 * d_model].astype(BF16)
        pos_perm = positions.reshape(batch, seq // dil, dil).transpose(0, 2, 1).reshape(1, -1)
        qkvs.append(_dil_inproj(hs[g].reshape(batch * seq, d_model), pos_perm, inv, wqk, wv,
                                _col(q_gain[g]), _col(k_gain[g]), tt=tt, name=f"dil_inproj_d{dil}"))
    head_of_col = np.arange(d_model) // HEAD_DIM
    expand = (np.arange(2 * HEAD_DIM)[:, None] == head_of_col[None, :]).astype(np.float32)
    expand = jnp.asarray(np.concatenate([expand, expand], axis=0), BF16)
    assert n_heads <= 2 * HEAD_DIM
    w_out = w_out.astype(BF16)

    def attend_and_merge(bounded):
        outs, stats = [], []
        for (q, kt, v), dil in zip(qkvs, dils):
            o, stat = _dil_attn(q, kt, v, n_seqs=batch * dil, tt=tt, bounded=bounded,
                                name=f"dil_attn_d{dil}" + ("" if bounded else "_exact"))
            outs.append(o)
            stats.append(stat)
        return _dil_out(outs, stats, expand, w_out, x2, batch=batch, seq=seq, tt=tt, dils=dils,
                        bounded=bounded)

    bound = (math.sqrt(HEAD_DIM) * LOG2E * (1.0 + 2.0 ** -6)
             * jnp.max(jnp.max(jnp.abs(q_gain), axis=1) * jnp.max(jnp.abs(k_gain), axis=1)))
    return lax.cond(bound <= DIL_MAX_SCORE_LOG2, functools.partial(attend_and_merge, True),
                    functools.partial(attend_and_merge, False))


def _forward(x, positions, fox_norm, fox_w_in, fox_b_f, fox_q_gain, fox_k_gain, fox_w_out,
             dil_norm, dil_w_in, dil_q_gain, dil_k_gain, dil_w_out, mlp_norm, mlp_w1, mlp_w2,
             *, tt, tk_fox, unroll_fox, tm, tf):
    batch, seq, d_model = x.shape
    assert seq % (tt * max(dil for _, dil in DIL_GROUPS)) == 0 and tt % tk_fox == 0
    x2 = x.reshape(batch * seq, d_model)
    depth = mlp_norm.shape[0]
    for layer in range(depth):
        j = layer // 2
        if layer % 2 == 0:
            x2 = _fox_layer(x2, fox_norm[j], fox_w_in[j], fox_b_f[j], fox_q_gain[j], fox_k_gain[j],
                            fox_w_out[j], batch=batch, seq=seq, tt=tt, tk=tk_fox, unroll=unroll_fox)
        else:
            x2 = _dil_layer(x2, positions, dil_norm[j], dil_w_in[j], dil_q_gain[j], dil_k_gain[j],
                            dil_w_out[j], batch=batch, seq=seq, tt=tt)
        x2 = _mlp(x2, mlp_norm[layer].reshape(1, -1), mlp_w1[layer].astype(BF16),
                  mlp_w2[layer].astype(BF16), tm=tm, tf=tf)
    return x2.reshape(batch, seq, d_model)


def kernel(x, positions, fox_norm, fox_w_in, fox_b_f, fox_q_gain, fox_k_gain, fox_w_out, dil_norm, dil_w_in, dil_q_gain, dil_k_gain, dil_w_out, mlp_norm, mlp_w1, mlp_w2):
    return _forward(x, positions, fox_norm, fox_w_in, fox_b_f, fox_q_gain, fox_k_gain, fox_w_out,
                    dil_norm, dil_w_in, dil_q_gain, dil_k_gain, dil_w_out, mlp_norm, mlp_w1, mlp_w2,
                    tt=512, tk_fox=256, unroll_fox=6, tm=1024, tf=2048)
```

```python
import functools
import math

import jax
import jax.numpy as jnp
import numpy as np
from jax import lax
from jax.experimental import pallas as pl
from jax.experimental.pallas import tpu as pltpu

F32, BF16 = jnp.float32, jnp.bfloat16
HEAD_DIM = 64
ROT_DIM = 16
ROPE_THETA = 500000.0
DIL_GROUPS = ((128, 1), (512, 4), (2048, 16))
DIL_BLOCK = 128
EPS = 1e-6
NEG_INF = -1e30
LOG2E = 1.4426950408889634
LANES = 128
BIAS_ROWS = 3
FOX_SKIP_LOG2 = 160.0
FOX_MAX_LOOSENESS_LOG2 = 100.0
DIL_MAX_SCORE_LOG2 = 50.0
VMEM_LIMIT = 56 * 1024 * 1024

_NT = (((1,), (1,)), ((), ()))
_TN = (((0,), (0,)), ((), ()))


def _params(sem):
    return pltpu.CompilerParams(dimension_semantics=sem, vmem_limit_bytes=VMEM_LIMIT)


def _rms(x, g):
    return x * lax.rsqrt(jnp.mean(x * x, axis=-1, keepdims=True) + EPS) * g


def _head_norm(y, gain):
    inv = lax.rsqrt(jnp.mean(y * y, axis=1, keepdims=True) + EPS)
    return y * inv * gain[None]


def _split3(x):
    hi = x.astype(BF16).astype(F32)
    rem = x - hi
    mid = rem.astype(BF16).astype(F32)
    return hi, mid, rem - mid


def _fox_inproj_kernel(x_ref, g_ref, wt_ref, wf_ref, bf_ref, gq_ref, gk_ref,
                       qkv_ref, c_ref, carry_ref, *, tiles_per_batch):
    i = pl.program_id(0)

    @pl.when(i % tiles_per_batch == 0)
    def _():
        carry_ref[...] = jnp.zeros_like(carry_ref)

    h = _rms(x_ref[...], g_ref[...]).astype(BF16)
    tt, d_model = x_ref.shape
    n_heads = d_model // HEAD_DIM

    f = lax.dot_general(wf_ref[...], h, _NT, preferred_element_type=F32) + bf_ref[...]
    lf = (jnp.minimum(f, 0.0) - jnp.log1p(jnp.exp(-jnp.abs(f)))) * LOG2E
    lane = lax.broadcasted_iota(jnp.int32, lf.shape, 1)
    shift = 1
    while shift < tt:
        lf = lf + jnp.where(lane >= shift, pltpu.roll(lf, shift, 1), 0.0)
        shift *= 2
    c = lf + carry_ref[:, 0:1]
    c_ref[...] = c
    carry_ref[...] = jnp.broadcast_to(c[:, tt - 1:tt], carry_ref.shape)

    for part, gain_ref, mult in ((0, gq_ref, LOG2E / math.sqrt(HEAD_DIM)), (1, gk_ref, 1.0)):
        rows = slice(part * d_model, (part + 1) * d_model)
        y = lax.dot_general(wt_ref[rows, :], h, _NT, preferred_element_type=F32)
        y = _head_norm(y.reshape(n_heads, HEAD_DIM, tt), gain_ref[...] * mult)
        qkv_ref[rows, :] = y.reshape(d_model, tt).astype(BF16)
    rows = slice(2 * d_model, 3 * d_model)
    qkv_ref[rows, :] = lax.dot_general(wt_ref[rows, :], h, _NT, preferred_element_type=F32).astype(BF16)


def _fox_inproj(x2, g, wt, wf, bf, gq, gk, *, seq, tt):
    tokens, d_model = x2.shape
    n_heads = d_model // HEAD_DIM
    nt = tokens // tt
    const = lambda i: (0, 0)
    return pl.pallas_call(
        functools.partial(_fox_inproj_kernel, tiles_per_batch=seq // tt),
        grid=(nt,),
        in_specs=[
            pl.BlockSpec((tt, d_model), lambda i: (i, 0)),
            pl.BlockSpec((1, d_model), const),
            pl.BlockSpec((3 * d_model, d_model), const),
            pl.BlockSpec((n_heads, d_model), const),
            pl.BlockSpec((n_heads, 1), const),
            pl.BlockSpec((HEAD_DIM, 1), const),
            pl.BlockSpec((HEAD_DIM, 1), const),
        ],
        out_specs=[
            pl.BlockSpec((None, 3 * d_model, tt), lambda i: (i, 0, 0)),
            pl.BlockSpec((None, n_heads, tt), lambda i: (i, 0, 0)),
        ],
        out_shape=[
            jax.ShapeDtypeStruct((nt, 3 * d_model, tt), BF16),
            jax.ShapeDtypeStruct((nt, n_heads, tt), F32),
        ],
        scratch_shapes=[pltpu.VMEM((n_heads, LANES), F32)],
        compiler_params=_params(("arbitrary",)),
        name="fox_inproj",
    )(x2, g, wt, wf, bf, gq, gk)


def _build_keys(k_ref, c_ref, kp_ref, extra_rows):
    tiles, _, tt = k_ref.shape
    pad = jnp.zeros((HEAD_DIM - BIAS_ROWS - extra_rows, tt), F32)
    ones = [jnp.ones((extra_rows, tt), F32)] if extra_rows else []
    for n in range(tiles):
        blk = jnp.concatenate([k_ref[n].astype(F32), *_split3(-c_ref[n]), *ones, pad], axis=0)
        kp_ref[n * tt:(n + 1) * tt, :] = blk.T.astype(BF16)


def _fox_exact_kernel(q_ref, k_ref, v_ref, c_ref, o_ref, kp_ref):
    i = pl.program_id(2)
    tt = q_ref.shape[1]

    @pl.when(i == 0)
    def _():
        _build_keys(k_ref, c_ref, kp_ref, 0)

    ones_rows = (lax.broadcasted_iota(jnp.int32, (HEAD_DIM, tt), 0) < BIAS_ROWS).astype(BF16)
    qp = jnp.concatenate([q_ref[...], ones_rows], axis=0)

    def step(j, carry, masked):
        m, l, acc = carry
        off = pl.multiple_of(j * tt, tt)
        s = jnp.dot(kp_ref[pl.ds(off, tt), :], qp, preferred_element_type=F32)
        if masked:
            key = lax.broadcasted_iota(jnp.int32, s.shape, 0)
            qry = lax.broadcasted_iota(jnp.int32, s.shape, 1)
            s = jnp.where(key <= qry, s, NEG_INF)
        m_new = jnp.maximum(m, jnp.max(s, axis=0, keepdims=True))
        alpha = jnp.exp2(m - m_new)
        p = jnp.exp2(s - m_new)
        l = alpha * l + jnp.sum(p, axis=0, keepdims=True)
        pv = jnp.dot(v_ref[j], p.astype(BF16), preferred_element_type=F32)
        return m_new, l, alpha * acc + pv

    init = (jnp.full((1, tt), NEG_INF, F32), jnp.zeros((1, tt), F32), jnp.zeros((HEAD_DIM, tt), F32))
    carry = lax.fori_loop(0, i, lambda j, c: step(j, c, False), init)
    m, l, acc = step(i, carry, True)
    o_ref[...] = (acc / l).astype(BF16)


def _fox_fast_kernel(jlo_ref, kb_ref, q_ref, k_ref, v_ref, c_ref, o_ref, kp_ref, qp_ref, *sd_refs, tk, unroll):
    bh = pl.program_id(0) * pl.num_programs(1) + pl.program_id(1)
    nq, _, tq = q_ref.shape
    per, nkv = tq // tk, nq * (tq // tk)
    s_refs, d_refs = sd_refs[:per], sd_refs[per:]
    kb = kb_ref[0]
    pad_rows = HEAD_DIM - 2 * BIAS_ROWS

    _build_keys(k_ref, c_ref, kp_ref, BIAS_ROWS)
    lane = lax.broadcasted_iota(jnp.int32, (tk, 2 * HEAD_DIM), 1)
    kp_ref[nq * tq:nq * tq + tk, :] = jnp.where(lane == HEAD_DIM, NEG_INF, 0.0).astype(BF16)

    ones_row = (lax.broadcasted_iota(jnp.int32, (16, tk), 0) == 0).astype(BF16)
    key = lax.broadcasted_iota(jnp.int32, (tk, tq), 0)
    qry = lax.broadcasted_iota(jnp.int32, (tk, tq), 1)

    def qk(blk, qp):
        off = pl.multiple_of(blk * tk, tk)
        return jnp.dot(kp_ref[pl.ds(off, tk), :], qp, preferred_element_type=F32)

    def prologue(i):
        qt = q_ref[i].astype(F32)
        qn = jnp.sqrt(jnp.sum(qt * qt, axis=0, keepdims=True))
        neg_m = c_ref[i] - qn * kb
        qp = jnp.concatenate([qt, jnp.ones((BIAS_ROWS, tq), F32), *_split3(neg_m),
                              jnp.zeros((pad_rows, tq), F32)], axis=0).astype(BF16)
        qp_ref[...] = qp
        for t in range(per):
            d_refs[t][...] = jnp.where(key + t * tk <= qry, qk(i * per + t, qp), NEG_INF)

    prologue(0)

    def q_block(i, _):
        qp = qp_ref[...]
        for t in range(per):
            s_refs[t][...] = d_refs[t][...]
        jlo = jlo_ref[bh * nq + i]
        n_off = i * per - jlo
        n_it = (n_off + unroll - 1) // unroll

        def pv(blk, s, acc):
            off = pl.multiple_of(lax.rem(blk, per) * tk, tk)
            p = jnp.exp2(s).astype(BF16)
            vv = jnp.concatenate([v_ref[lax.div(blk, per), :, pl.ds(off, tk)], ones_row], axis=0)
            return acc + jnp.dot(vv, p, preferred_element_type=F32)

        def off_tile(g):
            return jnp.where(g < n_off, jlo + g, nkv), jnp.where(g < n_off, jlo + g, 0)

        def consumed_v(g):
            return jnp.where(g < per, i * per + g, off_tile(g - per)[1])

        def body(u, acc):
            pend = [s_refs[t][...] for t in range(per)]
            for t in range(unroll):
                g = u * unroll + t
                pend.append(qk(off_tile(g)[0], qp))
                acc = pv(consumed_v(g), pend.pop(0), acc)
            for t in range(per):
                s_refs[t][...] = pend[t]
            return acc

        acc = lax.fori_loop(0, n_it, body, jnp.zeros((HEAD_DIM + 16, tq), F32))
        prologue(jnp.minimum(i + 1, nq - 1))
        for t in range(per):
            acc = pv(consumed_v(n_it * unroll + t), s_refs[t][...], acc)
        o_ref[i] = (acc[0:HEAD_DIM] / acc[HEAD_DIM:HEAD_DIM + 1]).astype(BF16)
        return 0

    lax.fori_loop(0, nq, q_block, 0)


def _fox_attn_fast(qkv, c4, jlo, kb, *, batch, seq, tk, unroll):
    nt, rows, tt = qkv.shape
    d_model = rows // 3
    n_heads = d_model // HEAD_DIM
    tpb = seq // tt
    grid_spec = pltpu.PrefetchScalarGridSpec(
        num_scalar_prefetch=1,
        grid=(batch, n_heads),
        in_specs=[
            pl.BlockSpec(memory_space=pltpu.SMEM),
            pl.BlockSpec((tpb, HEAD_DIM, tt), lambda b, h, jlo: (b, h, 0)),
            pl.BlockSpec((tpb, HEAD_DIM, tt), lambda b, h, jlo: (b, n_heads + h, 0)),
            pl.BlockSpec((tpb, HEAD_DIM, tt), lambda b, h, jlo: (b, 2 * n_heads + h, 0)),
            pl.BlockSpec((tpb, None, 1, tt), lambda b, h, jlo: (b, h, 0, 0)),
        ],
        out_specs=pl.BlockSpec((tpb, HEAD_DIM, tt), lambda b, h, jlo: (b, h, 0)),
        scratch_shapes=[pltpu.VMEM((seq + tk, 2 * HEAD_DIM), BF16), pltpu.VMEM((2 * HEAD_DIM, tt), BF16)]
        + [pltpu.VMEM((tk, tt), F32)] * (2 * (tt // tk)),
    )
    return pl.pallas_call(
        functools.partial(_fox_fast_kernel, tk=tk, unroll=unroll),
        grid_spec=grid_spec,
        out_shape=jax.ShapeDtypeStruct((nt, d_model, tt), BF16),
        compiler_params=_params(("arbitrary", "arbitrary")),
        name="fox_attn_fast",
    )(jlo, kb, qkv, qkv, qkv, c4)


def _fox_attn_exact(qkv, c4, *, batch, seq):
    nt, rows, tt = qkv.shape
    d_model = rows // 3
    n_heads = d_model // HEAD_DIM
    tpb = seq // tt
    return pl.pallas_call(
        _fox_exact_kernel,
        grid=(batch, n_heads, tpb),
        in_specs=[
            pl.BlockSpec((None, HEAD_DIM, tt), lambda b, h, i: (b * tpb + i, h, 0)),
            pl.BlockSpec((tpb, HEAD_DIM, tt), lambda b, h, i: (b, n_heads + h, 0)),
            pl.BlockSpec((tpb, HEAD_DIM, tt), lambda b, h, i: (b, 2 * n_heads + h, 0)),
            pl.BlockSpec((tpb, None, 1, tt), lambda b, h, i: (b, h, 0, 0)),
        ],
        out_specs=pl.BlockSpec((None, HEAD_DIM, tt), lambda b, h, i: (b * tpb + i, h, 0)),
        out_shape=jax.ShapeDtypeStruct((nt, d_model, tt), BF16),
        scratch_shapes=[pltpu.VMEM((seq, 2 * HEAD_DIM), BF16)],
        compiler_params=_params(("arbitrary", "arbitrary", "arbitrary")),
        name="fox_attn_exact",
    )(qkv, qkv, qkv, c4)


def _outproj_t_kernel(ot_ref, w_ref, x_ref, y_ref):
    y_ref[...] = x_ref[...] + lax.dot_general(ot_ref[...], w_ref[...], _TN, preferred_element_type=F32)


def _outproj_t(ot, w, x2):
    nt, d_model, tt = ot.shape
    return pl.pallas_call(
        _outproj_t_kernel,
        grid=(nt,),
        in_specs=[
            pl.BlockSpec((None, d_model, tt), lambda i: (i, 0, 0)),
            pl.BlockSpec((d_model, d_model), lambda i: (0, 0)),
            pl.BlockSpec((tt, d_model), lambda i: (i, 0)),
        ],
        out_specs=pl.BlockSpec((tt, d_model), lambda i: (i, 0)),
        out_shape=jax.ShapeDtypeStruct(x2.shape, F32),
        compiler_params=_params(("arbitrary",)),
        name="fox_outproj",
    )(ot, w, x2)


def _mlp_kernel(x_ref, g_ref, w1_ref, w2_ref, y_ref, h_ref, acc_ref):
    f = pl.program_id(1)

    @pl.when(f == 0)
    def _():
        h_ref[...] = _rms(x_ref[...], g_ref[...]).astype(BF16)
        acc_ref[...] = jnp.zeros_like(acc_ref)

    a = jnp.dot(h_ref[...], w1_ref[...], preferred_element_type=F32)
    a = jnp.square(jnp.maximum(a, 0.0)).astype(BF16)
    acc_ref[...] += jnp.dot(a, w2_ref[...], preferred_element_type=F32)

    @pl.when(f == pl.num_programs(1) - 1)
    def _():
        y_ref[...] = x_ref[...] + acc_ref[...]


def _mlp(x2, g, w1, w2, *, tm, tf):
    tokens, d_model = x2.shape
    d_ff = w1.shape[1]
    return pl.pallas_call(
        _mlp_kernel,
        grid=(tokens // tm, d_ff // tf),
        in_specs=[
            pl.BlockSpec((tm, d_model), lambda i, f: (i, 0)),
            pl.BlockSpec((1, d_model), lambda i, f: (0, 0)),
            pl.BlockSpec((d_model, tf), lambda i, f: (0, f)),
            pl.BlockSpec((tf, d_model), lambda i, f: (f, 0)),
        ],
        out_specs=pl.BlockSpec((tm, d_model), lambda i, f: (i, 0)),
        out_shape=jax.ShapeDtypeStruct((tokens, d_model), F32),
        scratch_shapes=[pltpu.VMEM((tm, d_model), BF16), pltpu.VMEM((tm, d_model), F32)],
        compiler_params=_params(("arbitrary", "arbitrary")),
        name="mlp",
    )(x2, g, w1, w2)


def _dil_prep_kernel(x_ref, g_ref, *refs, dils):
    out_refs, slab_ref = refs[:-1], refs[-1]
    h = _rms(x_ref[...], g_ref[...])
    tt, d_model = h.shape
    n_slabs = d_model // LANES
    if any(d > 1 for d in dils):
        for j in range(n_slabs):
            slab_ref[j] = h[:, j * LANES:(j + 1) * LANES]
    for dil, o_ref in zip(dils, out_refs):
        if dil == 1:
            o_ref[...] = h.astype(BF16)
            continue
        n = tt // dil
        for r in range(dil):
            rows = [slab_ref[j, pl.ds(r, n, stride=dil), :] for j in range(n_slabs)]
            o_ref[r] = jnp.concatenate(rows, axis=1).astype(BF16)


def _dil_prep(x2, g, *, batch, seq, tt, dils):
    tokens, d_model = x2.shape
    tpb = seq // tt
    out_specs, out_shape = [], []
    for dil in dils:
        if dil == 1:
            out_specs.append(pl.BlockSpec((tt, d_model), lambda i: (i, 0)))
            out_shape.append(jax.ShapeDtypeStruct((tokens, d_model), BF16))
        else:
            out_specs.append(pl.BlockSpec((None, dil, tt // dil, d_model), lambda i: (i // tpb, 0, i % tpb, 0)))
            out_shape.append(jax.ShapeDtypeStruct((batch, dil, seq // dil, d_model), BF16))
    return pl.pallas_call(
        functools.partial(_dil_prep_kernel, dils=dils),
        grid=(tokens // tt,),
        in_specs=[pl.BlockSpec((tt, d_model), lambda i: (i, 0)), pl.BlockSpec((1, d_model), lambda i: (0, 0))],
        out_specs=out_specs,
        out_shape=out_shape,
        scratch_shapes=[pltpu.VMEM((d_model // LANES, tt, LANES), F32)],
        compiler_params=_params(("arbitrary",)),
        name="dil_prep",
    )(x2, g)


_TWO_PI = 2.0 * math.pi
_PI2_A = 6.28125
_PI2_B = float((np.array(_TWO_PI - _PI2_A, np.float32).view(np.uint32) & np.uint32(0xFFFFF000)).view(np.float32))
_PI2_C = _TWO_PI - _PI2_A - _PI2_B


def _rope_tables(pos_row, inv_col):
    ang = pos_row.astype(F32) * inv_col
    k = jnp.floor(ang * (1.0 / _TWO_PI) + 0.5)
    r = ((ang - k * _PI2_A) - k * _PI2_B) - k * _PI2_C
    return jnp.cos(r), jnp.sin(r)


def _dil_inproj_kernel(h_ref, pos_ref, inv_ref, wqk_ref, wv_ref, gq_ref, gk_ref, q_ref, kt_ref, v_ref):
    h = h_ref[...]
    tt, d_model = h.shape
    n_heads = d_model // HEAD_DIM
    half = ROT_DIM // 2
    cos, sin = _rope_tables(pos_ref[...], inv_ref[...])
    qk = lax.dot_general(wqk_ref[...], h, _NT, preferred_element_type=F32)
    for part, gain_ref, mult in ((0, gq_ref, LOG2E / math.sqrt(HEAD_DIM)), (1, gk_ref, 1.0)):
        rows = slice(part * d_model, (part + 1) * d_model)
        y = _head_norm(qk[rows].reshape(n_heads, HEAD_DIM, tt), gain_ref[...] * mult)
        x1, x2 = y[:, 0:half], y[:, half:ROT_DIM]
        y = jnp.concatenate([x1 * cos - x2 * sin, x2 * cos + x1 * sin, y[:, ROT_DIM:]], axis=1)
        y = y.reshape(d_model, tt)
        if part == 0:
            q_ref[...] = y.T.astype(BF16)
        else:
            kt_ref[...] = y.astype(BF16)
    v_ref[...] = jnp.dot(h, wv_ref[...], preferred_element_type=F32).astype(BF16)


def _dil_inproj(h2, pos_perm, inv, wqk, wv, gq, gk, *, tt, name):
    tokens, d_model = h2.shape
    nt = tokens // tt
    const = lambda i: (0, 0)
    return pl.pallas_call(
        _dil_inproj_kernel,
        grid=(nt,),
        in_specs=[
            pl.BlockSpec((tt, d_model), lambda i: (i, 0)),
            pl.BlockSpec((1, tt), lambda i: (0, i)),
            pl.BlockSpec((ROT_DIM // 2, 1), const),
            pl.BlockSpec((2 * d_model, d_model), const),
            pl.BlockSpec((d_model, d_model), const),
            pl.BlockSpec((HEAD_DIM, 1), const),
            pl.BlockSpec((HEAD_DIM, 1), const),
        ],
        out_specs=[
            pl.BlockSpec((tt, d_model), lambda i: (i, 0)),
            pl.BlockSpec((None, d_model, tt), lambda i: (i, 0, 0)),
            pl.BlockSpec((tt, d_model), lambda i: (i, 0)),
        ],
        out_shape=[
            jax.ShapeDtypeStruct((tokens, d_model), BF16),
            jax.ShapeDtypeStruct((nt, d_model, tt), BF16),
            jax.ShapeDtypeStruct((tokens, d_model), BF16),
        ],
        compiler_params=_params(("arbitrary",)),
        name=name,
    )(h2, pos_perm, inv, wqk, wv, gq, gk)


def _dil_attn_kernel(q_ref, ktp_ref, ktc_ref, vp_ref, vc_ref, o_ref, stat_ref, *, bounded):
    i = pl.program_id(1)
    blk = DIL_BLOCK
    has_prev = i > 0
    tq, d_model = q_ref.shape
    kt_all = jnp.concatenate([ktp_ref[...], ktc_ref[...]], axis=1)
    v_all = jnp.concatenate([vp_ref[...], vc_ref[...]], axis=0)
    r = lax.broadcasted_iota(jnp.int32, (blk, 2 * blk), 0)
    c = lax.broadcasted_iota(jnp.int32, (blk, 2 * blk), 1)
    band = (c >= r) & (c <= r + blk)
    first_mask = band & (has_prev | (c >= blk))
    lane = lax.broadcasted_iota(jnp.int32, (blk, 2 * HEAD_DIM), 1)
    low = lane < HEAD_DIM
    for u in range(tq // blk):
        rows = slice(u * blk, (u + 1) * blk)
        mask = first_mask if u == 0 else band
        stat_tile = jnp.zeros((blk, 2 * HEAD_DIM), F32)
        for pair in range(d_model // (2 * HEAD_DIM)):
            cols = slice(pair * 2 * HEAD_DIM, (pair + 1) * 2 * HEAD_DIM)
            qp = q_ref[rows, cols]
            kt = kt_all[cols, u * blk:(u + 2) * blk]
            vv = v_all[u * blk:(u + 2) * blk, cols]
            outs = []
            for hh, sel in ((0, low), (1, ~low)):
                qh = jnp.where(sel, qp, jnp.zeros_like(qp))
                s = jnp.dot(qh, kt, preferred_element_type=F32)
                s = jnp.where(mask, s, NEG_INF)
                if bounded:
                    p = jnp.exp2(s)
                    stat = jnp.sum(p, axis=1, keepdims=True)
                    outs.append(jnp.dot(p.astype(BF16), vv, preferred_element_type=F32))
                else:
                    m = jnp.max(s, axis=1, keepdims=True)
                    p = jnp.exp2(s - m)
                    den = jnp.sum(p, axis=1, keepdims=True)
                    stat = m + jnp.log2(den)
                    outs.append(jnp.dot(p.astype(BF16), vv, preferred_element_type=F32) * (1.0 / den))
                stat_tile = jnp.where(lane == 2 * pair + hh, stat, stat_tile)
            o_ref[rows, cols] = jnp.where(low, outs[0], outs[1]).astype(BF16)
        stat_ref[rows, :] = stat_tile


def _dil_attn(q, kt, v, *, n_seqs, tt, bounded, name):
    tokens, d_model = q.shape
    nq = tokens // tt // n_seqs
    per = tt // DIL_BLOCK
    tile = lambda s, i: s * nq + i
    prev_tile = lambda s, i: jnp.maximum(tile(s, i) - 1, 0)
    return pl.pallas_call(
        functools.partial(_dil_attn_kernel, bounded=bounded),
        grid=(n_seqs, nq),
        in_specs=[
            pl.BlockSpec((tt, d_model), lambda s, i: (tile(s, i), 0)),
            pl.BlockSpec((None, d_model, DIL_BLOCK), lambda s, i: (prev_tile(s, i), 0, per - 1)),
            pl.BlockSpec((None, d_model, tt), lambda s, i: (tile(s, i), 0, 0)),
            pl.BlockSpec((DIL_BLOCK, d_model), lambda s, i: ((prev_tile(s, i) + 1) * per - 1, 0)),
            pl.BlockSpec((tt, d_model), lambda s, i: (tile(s, i), 0)),
        ],
        out_specs=[
            pl.BlockSpec((tt, d_model), lambda s, i: (tile(s, i), 0)),
            pl.BlockSpec((tt, 2 * HEAD_DIM), lambda s, i: (tile(s, i), 0)),
        ],
        out_shape=[
            jax.ShapeDtypeStruct((tokens, d_model), BF16),
            jax.ShapeDtypeStruct((tokens, 2 * HEAD_DIM), F32),
        ],
        compiler_params=_params(("arbitrary", "arbitrary")),
        name=name,
    )(q, kt, kt, v, v)


def _dil_out_kernel(*refs, dils, bounded):
    n = len(dils)
    o_refs, stat_refs = refs[:n], refs[n:2 * n]
    e_ref, w_ref, x_ref, y_ref, o_slab, stat_slab = refs[2 * n:]
    tt, d_model = x_ref.shape
    n_slabs = d_model // LANES

    def natural(ref, dil, slab):
        if dil == 1:
            return ref[...].astype(F32)
        rows, width = tt // dil, ref.shape[-1]
        for r in range(dil):
            val = ref[r].astype(F32)
            for j in range(width // LANES):
                slab[j, pl.ds(r, rows, stride=dil), :] = val[:, j * LANES:(j + 1) * LANES]
        return jnp.concatenate([slab[j] for j in range(width // LANES)], axis=1)

    stats = [natural(ref, dil, stat_slab) for ref, dil in zip(stat_refs, dils)]
    outs = lambda g: natural(o_refs[g], dils[g], o_slab)

    def per_head_to_cols(w):
        hi = w.astype(BF16)
        lo = (w - hi.astype(F32)).astype(BF16)
        return jnp.dot(jnp.concatenate([hi, lo], axis=1), e_ref[...], preferred_element_type=F32)

    if bounded:
        lane = lax.broadcasted_iota(jnp.int32, stats[0].shape, 1)
        den = jnp.where(lane < d_model // HEAD_DIM, sum(stats[1:], stats[0]), 1.0)
        num = outs(0)
        for g in range(1, n):
            num = num + outs(g)
        merged = num * per_head_to_cols(1.0 / den)
    else:
        top = functools.reduce(jnp.maximum, stats)
        es = [jnp.exp2(l - top) for l in stats]
        den = sum(es[1:], es[0])
        merged = None
        for g in range(n):
            term = per_head_to_cols(es[g] / den) * outs(g)
            merged = term if merged is None else merged + term
    y_ref[...] = x_ref[...] + jnp.dot(merged.astype(BF16), w_ref[...], preferred_element_type=F32)


def _dil_out(outs, stats, expand, w, x2, *, batch, seq, tt, dils, bounded):
    tokens, d_model = x2.shape
    tpb = seq // tt

    def view_and_spec(a, dil):
        width = a.shape[-1]
        if dil == 1:
            return a, pl.BlockSpec((tt, width), lambda i: (i, 0))
        spec = pl.BlockSpec((None, dil, tt // dil, width), lambda i: (i // tpb, 0, i % tpb, 0))
        return a.reshape(batch, dil, seq // dil, width), spec

    args, specs = [], []
    for group in (outs, stats):
        for a, dil in zip(group, dils):
            a, spec = view_and_spec(a, dil)
            args.append(a)
            specs.append(spec)
    row = lambda i: (i, 0)
    const = lambda i: (0, 0)
    return pl.pallas_call(
        functools.partial(_dil_out_kernel, dils=dils, bounded=bounded),
        grid=(tokens // tt,),
        in_specs=specs + [
            pl.BlockSpec((4 * HEAD_DIM, d_model), const),
            pl.BlockSpec((d_model, d_model), const),
            pl.BlockSpec((tt, d_model), row),
        ],
        out_specs=pl.BlockSpec((tt, d_model), row),
        out_shape=jax.ShapeDtypeStruct((tokens, d_model), F32),
        scratch_shapes=[pltpu.VMEM((d_model // LANES, tt, LANES), F32), pltpu.VMEM((1, tt, LANES), F32)],
        compiler_params=_params(("arbitrary",)),
        name="dil_outproj" + ("" if bounded else "_exact"),
    )(*args, expand, w, x2)


def _col(v):
    return v.reshape(-1, 1).astype(F32)


def _fox_layer(x2, norm_g, w_in, b_f, q_gain, k_gain, w_out, *, batch, seq, tt, tk, unroll):
    d_model = x2.shape[1]
    n_heads = d_model // HEAD_DIM
    tpb = seq // tt
    wt = w_in[:, :3 * d_model].T.astype(BF16)
    wf = w_in[:, 3 * d_model:].T.astype(BF16)
    qkv, c = _fox_inproj(x2, norm_g.reshape(1, -1), wt, wf, _col(b_f), _col(q_gain), _col(k_gain),
                         seq=seq, tt=tt)
    c4 = c.reshape(batch * tpb, n_heads, 1, tt)

    slack = 1.0 + 2.0 ** -7
    qb = math.sqrt(HEAD_DIM) * jnp.max(jnp.abs(q_gain)) * (LOG2E / math.sqrt(HEAD_DIM)) * slack
    kb = math.sqrt(HEAD_DIM) * jnp.max(jnp.abs(k_gain)) * slack
    per = tt // tk
    ch = c.reshape(batch, tpb, n_heads, tt).transpose(0, 2, 1, 3).reshape(batch, n_heads, seq)
    gap = ch[:, :, ::tt, None] - ch[:, :, None, tk - 1::tk]
    below = jnp.arange(seq // tk)[None, :] < (jnp.arange(tpb) * per)[:, None]
    jlo = jnp.sum((gap < -FOX_SKIP_LOG2) & below, axis=-1, dtype=jnp.int32).reshape(-1)

    ot = lax.cond(
        2.0 * qb * kb <= FOX_MAX_LOOSENESS_LOG2,
        lambda: _fox_attn_fast(qkv, c4, jlo, kb.reshape(1).astype(F32), batch=batch, seq=seq,
                               tk=tk, unroll=unroll),
        lambda: _fox_attn_exact(qkv, c4, batch=batch, seq=seq),
    )
    return _outproj_t(ot, w_out.astype(BF16), x2)


def _dil_layer(x2, positions, norm_g, w_in, q_gain, k_gain, w_out, *, batch, seq, tt):
    d_model = x2.shape[1]
    n_heads = d_model // HEAD_DIM
    dils = tuple(dil for _, dil in DIL_GROUPS)
    assert all(window // dil == DIL_BLOCK for window, dil in DIL_GROUPS)
    inv = (ROPE_THETA ** (-np.arange(ROT_DIM // 2, dtype=np.float64) * 2.0 / ROT_DIM)).astype(np.float32)
    inv = jnp.asarray(inv).reshape(-1, 1)
    hs = _dil_prep(x2, norm_g.reshape(1, -1), batch=batch, seq=seq, tt=tt, dils=dils)
    qkvs = []
    for g, dil in enumerate(dils):
        base = g * 3 * d_model
        wqk = w_in[:, base:base + 2 * d_model].T.astype(BF16)
        wv = w_in[:, base + 2 * d_model:base + 3 * d_model].astype(BF16)
        pos_perm = positions.reshape(batch, seq // dil, dil).transpose(0, 2, 1).reshape(1, -1)
        qkvs.append(_dil_inproj(hs[g].reshape(batch * seq, d_model), pos_perm, inv, wqk, wv,
                                _col(q_gain[g]), _col(k_gain[g]), tt=tt, name=f"dil_inproj_d{dil}"))
    head_of_col = np.arange(d_model) // HEAD_DIM
    expand = (np.arange(2 * HEAD_DIM)[:, None] == head_of_col[None, :]).astype(np.float32)
    expand = jnp.asarray(np.concatenate([expand, expand], axis=0), BF16)
    assert n_heads <= 2 * HEAD_DIM
    w_out = w_out.astype(BF16)

    def attend_and_merge(bounded):
        outs, stats = [], []
        for (q, kt, v), dil in zip(qkvs, dils):
            o, stat = _dil_attn(q, kt, v, n_seqs=batch * dil, tt=tt, bounded=bounded,
                                name=f"dil_attn_d{dil}" + ("" if bounded else "_exact"))
            outs.append(o)
            stats.append(stat)
        return _dil_out(outs, stats, expand, w_out, x2, batch=batch, seq=seq, tt=tt, dils=dils,
                        bounded=bounded)

    bound = (math.sqrt(HEAD_DIM) * LOG2E * (1.0 + 2.0 ** -6)
             * jnp.max(jnp.max(jnp.abs(q_gain), axis=1) * jnp.max(jnp.abs(k_gain), axis=1)))
    return lax.cond(bound <= DIL_MAX_SCORE_LOG2, functools.partial(attend_and_merge, True),
                    functools.partial(attend_and_merge, False))


def _forward(x, positions, fox_norm, fox_w_in, fox_b_f, fox_q_gain, fox_k_gain, fox_w_out,
             dil_norm, dil_w_in, dil_q_gain, dil_k_gain, dil_w_out, mlp_norm, mlp_w1, mlp_w2,
             *, tt, tk_fox, unroll_fox, tm, tf):
    batch, seq, d_model = x.shape
    assert seq % (tt * max(dil for _, dil in DIL_GROUPS)) == 0 and tt % tk_fox == 0
    x2 = x.reshape(batch * seq, d_model)
    depth = mlp_norm.shape[0]
    for layer in range(depth):
        j = layer // 2
        if layer % 2 == 0:
            x2 = _fox_layer(x2, fox_norm[j], fox_w_in[j], fox_b_f[j], fox_q_gain[j], fox_k_gain[j],
                            fox_w_out[j], batch=batch, seq=seq, tt=tt, tk=tk_fox, unroll=unroll_fox)
        else:
            x2 = _dil_layer(x2, positions, dil_norm[j], dil_w_in[j], dil_q_gain[j], dil_k_gain[j],
                            dil_w_out[j], batch=batch, seq=seq, tt=tt)
        x2 = _mlp(x2, mlp_norm[layer].reshape(1, -1), mlp_w1[layer].astype(BF16),
                  mlp_w2[layer].astype(BF16), tm=tm, tf=tf)
    return x2.reshape(batch, seq, d_model)


def kernel(x, positions, fox_norm, fox_w_in, fox_b_f, fox_q_gain, fox_k_gain, fox_w_out, dil_norm, dil_w_in, dil_q_gain, dil_k_gain, dil_w_out, mlp_norm, mlp_w1, mlp_w2):
    return _forward(x, positions, fox_norm, fox_w_in, fox_b_f, fox_q_gain, fox_k_gain, fox_w_out,
                    dil_norm, dil_w_in, dil_q_gain, dil_k_gain, dil_w_out, mlp_norm, mlp_w1, mlp_w2,
                    tt=512, tk_fox=256, unroll_fox=6, tm=1024, tf=2048)
```

```python
import functools
import math

import jax
import jax.numpy as jnp
import numpy as np
from jax import lax
from jax.experimental import pallas as pl
from jax.experimental.pallas import tpu as pltpu

F32, BF16 = jnp.float32, jnp.bfloat16
HEAD_DIM = 64
ROT_DIM = 16
ROPE_THETA = 500000.0
DIL_GROUPS = ((128, 1), (512, 4), (2048, 16))
DIL_BLOCK = 128
EPS = 1e-6
NEG_INF = -1e30
LOG2E = 1.4426950408889634
LANES = 128
BIAS_ROWS = 3
FOX_SKIP_LOG2 = 160.0
FOX_MAX_LOOSENESS_LOG2 = 100.0
DIL_MAX_SCORE_LOG2 = 50.0
VMEM_LIMIT = 56 * 1024 * 1024

_NT = (((1,), (1,)), ((), ()))
_TN = (((0,), (0,)), ((), ()))


def _params(sem):
    return pltpu.CompilerParams(dimension_semantics=sem, vmem_limit_bytes=VMEM_LIMIT)


def _rms(x, g):
    return x * lax.rsqrt(jnp.mean(x * x, axis=-1, keepdims=True) + EPS) * g


def _head_norm(y, gain):
    inv = lax.rsqrt(jnp.mean(y * y, axis=1, keepdims=True) + EPS)
    return y * inv * gain[None]


def _split3(x):
    hi = x.astype(BF16).astype(F32)
    rem = x - hi
    mid = rem.astype(BF16).astype(F32)
    return hi, mid, rem - mid


def _fox_inproj_kernel(x_ref, g_ref, wt_ref, wf_ref, bf_ref, gq_ref, gk_ref,
                       qkv_ref, c_ref, carry_ref, *, tiles_per_batch):
    i = pl.program_id(0)

    @pl.when(i % tiles_per_batch == 0)
    def _():
        carry_ref[...] = jnp.zeros_like(carry_ref)

    h = _rms(x_ref[...], g_ref[...]).astype(BF16)
    tt, d_model = x_ref.shape
    n_heads = d_model // HEAD_DIM

    f = lax.dot_general(wf_ref[...], h, _NT, preferred_element_type=F32) + bf_ref[...]
    lf = (jnp.minimum(f, 0.0) - jnp.log1p(jnp.exp(-jnp.abs(f)))) * LOG2E
    lane = lax.broadcasted_iota(jnp.int32, lf.shape, 1)
    shift = 1
    while shift < tt:
        lf = lf + jnp.where(lane >= shift, pltpu.roll(lf, shift, 1), 0.0)
        shift *= 2
    c = lf + carry_ref[:, 0:1]
    c_ref[...] = c
    carry_ref[...] = jnp.broadcast_to(c[:, tt - 1:tt], carry_ref.shape)

    for part, gain_ref, mult in ((0, gq_ref, LOG2E / math.sqrt(HEAD_DIM)), (1, gk_ref, 1.0)):
        rows = slice(part * d_model, (part + 1) * d_model)
        y = lax.dot_general(wt_ref[rows, :], h, _NT, preferred_element_type=F32)
        y = _head_norm(y.reshape(n_heads, HEAD_DIM, tt), gain_ref[...] * mult)
        qkv_ref[rows, :] = y.reshape(d_model, tt).astype(BF16)
    rows = slice(2 * d_model, 3 * d_model)
    qkv_ref[rows, :] = lax.dot_general(wt_ref[rows, :], h, _NT, preferred_element_type=F32).astype(BF16)


def _fox_inproj(x2, g, wt, wf, bf, gq, gk, *, seq, tt):
    tokens, d_model = x2.shape
    n_heads = d_model // HEAD_DIM
    nt = tokens // tt
    const = lambda i: (0, 0)
    return pl.pallas_call(
        functools.partial(_fox_inproj_kernel, tiles_per_batch=seq // tt),
        grid=(nt,),
        in_specs=[
            pl.BlockSpec((tt, d_model), lambda i: (i, 0)),
            pl.BlockSpec((1, d_model), const),
            pl.BlockSpec((3 * d_model, d_model), const),
            pl.BlockSpec((n_heads, d_model), const),
            pl.BlockSpec((n_heads, 1), const),
            pl.BlockSpec((HEAD_DIM, 1), const),
            pl.BlockSpec((HEAD_DIM, 1), const),
        ],
        out_specs=[
            pl.BlockSpec((None, 3 * d_model, tt), lambda i: (i, 0, 0)),
            pl.BlockSpec((None, n_heads, tt), lambda i: (i, 0, 0)),
        ],
        out_shape=[
            jax.ShapeDtypeStruct((nt, 3 * d_model, tt), BF16),
            jax.ShapeDtypeStruct((nt, n_heads, tt), F32),
        ],
        scratch_shapes=[pltpu.VMEM((n_heads, LANES), F32)],
        compiler_params=_params(("arbitrary",)),
        name="fox_inproj",
    )(x2, g, wt, wf, bf, gq, gk)


def _build_keys(k_ref, c_ref, kp_ref, extra_rows):
    tiles, _, tt = k_ref.shape
    pad = jnp.zeros((HEAD_DIM - BIAS_ROWS - extra_rows, tt), F32)
    ones = [jnp.ones((extra_rows, tt), F32)] if extra_rows else []
    for n in range(tiles):
        blk = jnp.concatenate([k_ref[n].astype(F32), *_split3(-c_ref[n]), *ones, pad], axis=0)
        kp_ref[n * tt:(n + 1) * tt, :] = blk.T.astype(BF16)


def _fox_exact_kernel(q_ref, k_ref, v_ref, c_ref, o_ref, kp_ref):
    i = pl.program_id(2)
    tt = q_ref.shape[1]

    @pl.when(i == 0)
    def _():
        _build_keys(k_ref, c_ref, kp_ref, 0)

    ones_rows = (lax.broadcasted_iota(jnp.int32, (HEAD_DIM, tt), 0) < BIAS_ROWS).astype(BF16)
    qp = jnp.concatenate([q_ref[...], ones_rows], axis=0)

    def step(j, carry, masked):
        m, l, acc = carry
        off = pl.multiple_of(j * tt, tt)
        s = jnp.dot(kp_ref[pl.ds(off, tt), :], qp, preferred_element_type=F32)
        if masked:
            key = lax.broadcasted_iota(jnp.int32, s.shape, 0)
            qry = lax.broadcasted_iota(jnp.int32, s.shape, 1)
            s = jnp.where(key <= qry, s, NEG_INF)
        m_new = jnp.maximum(m, jnp.max(s, axis=0, keepdims=True))
        alpha = jnp.exp2(m - m_new)
        p = jnp.exp2(s - m_new)
        l = alpha * l + jnp.sum(p, axis=0, keepdims=True)
        pv = jnp.dot(v_ref[j], p.astype(BF16), preferred_element_type=F32)
        return m_new, l, alpha * acc + pv

    init = (jnp.full((1, tt), NEG_INF, F32), jnp.zeros((1, tt), F32), jnp.zeros((HEAD_DIM, tt), F32))
    carry = lax.fori_loop(0, i, lambda j, c: step(j, c, False), init)
    m, l, acc = step(i, carry, True)
    o_ref[...] = (acc / l).astype(BF16)


def _fox_fast_kernel(jlo_ref, kb_ref, q_ref, k_ref, v_ref, c_ref, o_ref, kp_ref, qp_ref, *sd_refs, tk, unroll):
    bh = pl.program_id(0) * pl.num_programs(1) + pl.program_id(1)
    nq, _, tq = q_ref.shape
    per, nkv = tq // tk, nq * (tq // tk)
    s_refs, d_refs = sd_refs[:per], sd_refs[per:]
    kb = kb_ref[0]
    pad_rows = HEAD_DIM - 2 * BIAS_ROWS

    _build_keys(k_ref, c_ref, kp_ref, BIAS_ROWS)
    lane = lax.broadcasted_iota(jnp.int32, (tk, 2 * HEAD_DIM), 1)
    kp_ref[nq * tq:nq * tq + tk, :] = jnp.where(lane == HEAD_DIM, NEG_INF, 0.0).astype(BF16)

    ones_row = (lax.broadcasted_iota(jnp.int32, (16, tk), 0) == 0).astype(BF16)
    key = lax.broadcasted_iota(jnp.int32, (tk, tq), 0)
    qry = lax.broadcasted_iota(jnp.int32, (tk, tq), 1)

    def qk(blk, qp):
        off = pl.multiple_of(blk * tk, tk)
        return jnp.dot(kp_ref[pl.ds(off, tk), :], qp, preferred_element_type=F32)

    def prologue(i):
        qt = q_ref[i].astype(F32)
        qn = jnp.sqrt(jnp.sum(qt * qt, axis=0, keepdims=True))
        neg_m = c_ref[i] - qn * kb
        qp = jnp.concatenate([qt, jnp.ones((BIAS_ROWS, tq), F32), *_split3(neg_m),
                              jnp.zeros((pad_rows, tq), F32)], axis=0).astype(BF16)
        qp_ref[...] = qp
        for t in range(per):
            d_refs[t][...] = jnp.where(key + t * tk <= qry, qk(i * per + t, qp), NEG_INF)

    prologue(0)

    def q_block(i, _):
        qp = qp_ref[...]
        for t in range(per):
            s_refs[t][...] = d_refs[t][...]
        jlo = jlo_ref[bh * nq + i]
        n_off = i * per - jlo
        n_it = (n_off + unroll - 1) // unroll

        def pv(blk, s, acc):
            off = pl.multiple_of(lax.rem(blk, per) * tk, tk)
            p = jnp.exp2(s).astype(BF16)
            vv = jnp.concatenate([v_ref[lax.div(blk, per), :, pl.ds(off, tk)], ones_row], axis=0)
            return acc + jnp.dot(vv, p, preferred_element_type=F32)

        def off_tile(g):
            return jnp.where(g < n_off, jlo + g, nkv), jnp.where(g < n_off, jlo + g, 0)

        def consumed_v(g):
            return jnp.where(g < per, i * per + g, off_tile(g - per)[1])

        def body(u, acc):
            pend = [s_refs[t][...] for t in range(per)]
            for t in range(unroll):
                g = u * unroll + t
                pend.append(qk(off_tile(g)[0], qp))
                acc = pv(consumed_v(g), pend.pop(0), acc)
            for t in range(per):
                s_refs[t][...] = pend[t]
            return acc

        acc = lax.fori_loop(0, n_it, body, jnp.zeros((HEAD_DIM + 16, tq), F32))
        prologue(jnp.minimum(i + 1, nq - 1))
        for t in range(per):
            acc = pv(consumed_v(n_it * unroll + t), s_refs[t][...], acc)
        o_ref[i] = (acc[0:HEAD_DIM] / acc[HEAD_DIM:HEAD_DIM + 1]).astype(BF16)
        return 0

    lax.fori_loop(0, nq, q_block, 0)


def _fox_attn_fast(qkv, c4, jlo, kb, *, batch, seq, tk, unroll):
    nt, rows, tt = qkv.shape
    d_model = rows // 3
    n_heads = d_model // HEAD_DIM
    tpb = seq // tt
    grid_spec = pltpu.PrefetchScalarGridSpec(
        num_scalar_prefetch=1,
        grid=(batch, n_heads),
        in_specs=[
            pl.BlockSpec(memory_space=pltpu.SMEM),
            pl.BlockSpec((tpb, HEAD_DIM, tt), lambda b, h, jlo: (b, h, 0)),
            pl.BlockSpec((tpb, HEAD_DIM, tt), lambda b, h, jlo: (b, n_heads + h, 0)),
            pl.BlockSpec((tpb, HEAD_DIM, tt), lambda b, h, jlo: (b, 2 * n_heads + h, 0)),
            pl.BlockSpec((tpb, None, 1, tt), lambda b, h, jlo: (b, h, 0, 0)),
        ],
        out_specs=pl.BlockSpec((tpb, HEAD_DIM, tt), lambda b, h, jlo: (b, h, 0)),
        scratch_shapes=[pltpu.VMEM((seq + tk, 2 * HEAD_DIM), BF16), pltpu.VMEM((2 * HEAD_DIM, tt), BF16)]
        + [pltpu.VMEM((tk, tt), F32)] * (2 * (tt // tk)),
    )
    return pl.pallas_call(
        functools.partial(_fox_fast_kernel, tk=tk, unroll=unroll),
        grid_spec=grid_spec,
        out_shape=jax.ShapeDtypeStruct((nt, d_model, tt), BF16),
        compiler_params=_params(("arbitrary", "arbitrary")),
        name="fox_attn_fast",
    )(jlo, kb, qkv, qkv, qkv, c4)


def _fox_attn_exact(qkv, c4, *, batch, seq):
    nt, rows, tt = qkv.shape
    d_model = rows // 3
    n_heads = d_model // HEAD_DIM
    tpb = seq // tt
    return pl.pallas_call(
        _fox_exact_kernel,
        grid=(batch, n_heads, tpb),
        in_specs=[
            pl.BlockSpec((None, HEAD_DIM, tt), lambda b, h, i: (b * tpb + i, h, 0)),
            pl.BlockSpec((tpb, HEAD_DIM, tt), lambda b, h, i: (b, n_heads + h, 0)),
            pl.BlockSpec((tpb, HEAD_DIM, tt), lambda b, h, i: (b, 2 * n_heads + h, 0)),
            pl.BlockSpec((tpb, None, 1, tt), lambda b, h, i: (b, h, 0, 0)),
        ],
        out_specs=pl.BlockSpec((None, HEAD_DIM, tt), lambda b, h, i: (b * tpb + i, h, 0)),
        out_shape=jax.ShapeDtypeStruct((nt, d_model, tt), BF16),
        scratch_shapes=[pltpu.VMEM((seq, 2 * HEAD_DIM), BF16)],
        compiler_params=_params(("arbitrary", "arbitrary", "arbitrary")),
        name="fox_attn_exact",
    )(qkv, qkv, qkv, c4)


def _resident(shape):
    return pl.BlockSpec(shape, lambda i: (0,) * len(shape), pipeline_mode=pl.Buffered(1))


def _mlp_residual(x1, g_ref, w1_ref, w2_ref):
    h = _rms(x1, g_ref[...]).astype(BF16)
    a = jnp.dot(h, w1_ref[...], preferred_element_type=F32)
    a = jnp.square(jnp.maximum(a, 0.0)).astype(BF16)
    return x1 + jnp.dot(a, w2_ref[...], preferred_element_type=F32)


def _mlp_specs(d_model, d_ff):
    return [_resident((1, d_model)), _resident((d_model, d_ff)), _resident((d_ff, d_model))]


def _fox_out_mlp_kernel(ot_ref, w_ref, x_ref, g_ref, w1_ref, w2_ref, y_ref):
    x1 = x_ref[...] + lax.dot_general(ot_ref[...], w_ref[...], _TN, preferred_element_type=F32)
    y_ref[...] = _mlp_residual(x1, g_ref, w1_ref, w2_ref)


def _fox_out_mlp(ot, w, x2, g, w1, w2):
    nt, d_model, tt = ot.shape
    return pl.pallas_call(
        _fox_out_mlp_kernel,
        grid=(nt,),
        in_specs=[
            pl.BlockSpec((None, d_model, tt), lambda i: (i, 0, 0)),
            _resident((d_model, d_model)),
            pl.BlockSpec((tt, d_model), lambda i: (i, 0)),
        ] + _mlp_specs(d_model, w1.shape[1]),
        out_specs=pl.BlockSpec((tt, d_model), lambda i: (i, 0)),
        out_shape=jax.ShapeDtypeStruct(x2.shape, F32),
        compiler_params=_params(("arbitrary",)),
        name="fox_outproj_mlp",
    )(ot, w, x2, g, w1, w2)


def _dil_prep_kernel(x_ref, g_ref, *refs, dils):
    out_refs, slab_ref = refs[:-1], refs[-1]
    h = _rms(x_ref[...], g_ref[...])
    tt, d_model = h.shape
    n_slabs = d_model // LANES
    if any(d > 1 for d in dils):
        for j in range(n_slabs):
            slab_ref[j] = h[:, j * LANES:(j + 1) * LANES]
    for dil, o_ref in zip(dils, out_refs):
        if dil == 1:
            o_ref[...] = h.astype(BF16)
            continue
        n = tt // dil
        for r in range(dil):
            rows = [slab_ref[j, pl.ds(r, n, stride=dil), :] for j in range(n_slabs)]
            o_ref[r] = jnp.concatenate(rows, axis=1).astype(BF16)


def _dil_prep(x2, g, *, batch, seq, tt, dils):
    tokens, d_model = x2.shape
    tpb = seq // tt
    out_specs, out_shape = [], []
    for dil in dils:
        if dil == 1:
            out_specs.append(pl.BlockSpec((tt, d_model), lambda i: (i, 0)))
            out_shape.append(jax.ShapeDtypeStruct((tokens, d_model), BF16))
        else:
            out_specs.append(pl.BlockSpec((None, dil, tt // dil, d_model), lambda i: (i // tpb, 0, i % tpb, 0)))
            out_shape.append(jax.ShapeDtypeStruct((batch, dil, seq // dil, d_model), BF16))
    return pl.pallas_call(
        functools.partial(_dil_prep_kernel, dils=dils),
        grid=(tokens // tt,),
        in_specs=[pl.BlockSpec((tt, d_model), lambda i: (i, 0)), pl.BlockSpec((1, d_model), lambda i: (0, 0))],
        out_specs=out_specs,
        out_shape=out_shape,
        scratch_shapes=[pltpu.VMEM((d_model // LANES, tt, LANES), F32)],
        compiler_params=_params(("arbitrary",)),
        name="dil_prep",
    )(x2, g)


_TWO_PI = 2.0 * math.pi
_PI2_A = 6.28125
_PI2_B = float((np.array(_TWO_PI - _PI2_A, np.float32).view(np.uint32) & np.uint32(0xFFFFF000)).view(np.float32))
_PI2_C = _TWO_PI - _PI2_A - _PI2_B


def _rope_tables(pos_row, inv_col):
    ang = pos_row.astype(F32) * inv_col
    k = jnp.floor(ang * (1.0 / _TWO_PI) + 0.5)
    r = ((ang - k * _PI2_A) - k * _PI2_B) - k * _PI2_C
    return jnp.cos(r), jnp.sin(r)


def _dil_inproj_kernel(h_ref, pos_ref, inv_ref, wqk_ref, wv_ref, gq_ref, gk_ref, q_ref, kt_ref, v_ref):
    h = h_ref[...]
    tt, d_model = h.shape
    n_heads = d_model // HEAD_DIM
    half = ROT_DIM // 2
    cos, sin = _rope_tables(pos_ref[...], inv_ref[...])
    for part, gain_ref, mult in ((0, gq_ref, LOG2E / math.sqrt(HEAD_DIM)), (1, gk_ref, 1.0)):
        rows = slice(part * d_model, (part + 1) * d_model)
        y = lax.dot_general(wqk_ref[rows, :], h, _NT, preferred_element_type=F32)
        y = _head_norm(y.reshape(n_heads, HEAD_DIM, tt), gain_ref[...] * mult)
        x1, x2 = y[:, 0:half], y[:, half:ROT_DIM]
        y = jnp.concatenate([x1 * cos - x2 * sin, x2 * cos + x1 * sin, y[:, ROT_DIM:]], axis=1)
        y = y.reshape(d_model, tt)
        if part == 0:
            q_ref[...] = y.T.astype(BF16)
        else:
            kt_ref[...] = y.astype(BF16)
    v_ref[...] = jnp.dot(h, wv_ref[...], preferred_element_type=F32).astype(BF16)


def _dil_inproj(h2, pos_perm, inv, wqk, wv, gq, gk, *, tt, name):
    tokens, d_model = h2.shape
    nt = tokens // tt
    const = lambda i: (0, 0)
    return pl.pallas_call(
        _dil_inproj_kernel,
        grid=(nt,),
        in_specs=[
            pl.BlockSpec((tt, d_model), lambda i: (i, 0)),
            pl.BlockSpec((1, tt), lambda i: (0, i)),
            pl.BlockSpec((ROT_DIM // 2, 1), const),
            pl.BlockSpec((2 * d_model, d_model), const),
            pl.BlockSpec((d_model, d_model), const),
            pl.BlockSpec((HEAD_DIM, 1), const),
            pl.BlockSpec((HEAD_DIM, 1), const),
        ],
        out_specs=[
            pl.BlockSpec((tt, d_model), lambda i: (i, 0)),
            pl.BlockSpec((None, d_model, tt), lambda i: (i, 0, 0)),
            pl.BlockSpec((tt, d_model), lambda i: (i, 0)),
        ],
        out_shape=[
            jax.ShapeDtypeStruct((tokens, d_model), BF16),
            jax.ShapeDtypeStruct((nt, d_model, tt), BF16),
            jax.ShapeDtypeStruct((tokens, d_model), BF16),
        ],
        compiler_params=_params(("arbitrary",)),
        name=name,
    )(h2, pos_perm, inv, wqk, wv, gq, gk)


def _dil_attn_kernel(q_ref, ktp_ref, ktc_ref, vp_ref, vc_ref, o_ref, stat_ref, *, bounded):
    i = pl.program_id(1)
    blk = DIL_BLOCK
    has_prev = i > 0
    tq, d_model = q_ref.shape
    kt_all = jnp.concatenate([ktp_ref[...], ktc_ref[...]], axis=1)
    v_all = jnp.concatenate([vp_ref[...], vc_ref[...]], axis=0)
    r = lax.broadcasted_iota(jnp.int32, (blk, 2 * blk), 0)
    c = lax.broadcasted_iota(jnp.int32, (blk, 2 * blk), 1)
    band = (c >= r) & (c <= r + blk)
    first_mask = band & (has_prev | (c >= blk))
    lane = lax.broadcasted_iota(jnp.int32, (blk, 2 * HEAD_DIM), 1)
    low = lane < HEAD_DIM
    for u in range(tq // blk):
        rows = slice(u * blk, (u + 1) * blk)
        mask = first_mask if u == 0 else band
        stat_tile = jnp.zeros((blk, 2 * HEAD_DIM), F32)
        for pair in range(d_model // (2 * HEAD_DIM)):
            cols = slice(pair * 2 * HEAD_DIM, (pair + 1) * 2 * HEAD_DIM)
            qp = q_ref[rows, cols]
            kt = kt_all[cols, u * blk:(u + 2) * blk]
            vv = v_all[u * blk:(u + 2) * blk, cols]
            outs = []
            for hh, sel in ((0, low), (1, ~low)):
                qh = jnp.where(sel, qp, jnp.zeros_like(qp))
                s = jnp.dot(qh, kt, preferred_element_type=F32)
                s = jnp.where(mask, s, NEG_INF)
                if bounded:
                    p = jnp.exp2(s)
                    stat = jnp.sum(p, axis=1, keepdims=True)
                    outs.append(jnp.dot(p.astype(BF16), vv, preferred_element_type=F32))
                else:
                    m = jnp.max(s, axis=1, keepdims=True)
                    p = jnp.exp2(s - m)
                    den = jnp.sum(p, axis=1, keepdims=True)
                    stat = m + jnp.log2(den)
                    outs.append(jnp.dot(p.astype(BF16), vv, preferred_element_type=F32) * (1.0 / den))
                stat_tile = jnp.where(lane == 2 * pair + hh, stat, stat_tile)
            o_ref[rows, cols] = jnp.where(low, outs[0], outs[1]).astype(BF16)
        stat_ref[rows, :] = stat_tile


def _dil_attn(q, kt, v, *, n_seqs, tt, bounded, name):
    tokens, d_model = q.shape
    nq = tokens // tt // n_seqs
    per = tt // DIL_BLOCK
    tile = lambda s, i: s * nq + i
    prev_tile = lambda s, i: jnp.maximum(tile(s, i) - 1, 0)
    return pl.pallas_call(
        functools.partial(_dil_attn_kernel, bounded=bounded),
        grid=(n_seqs, nq),
        in_specs=[
            pl.BlockSpec((tt, d_model), lambda s, i: (tile(s, i), 0)),
            pl.BlockSpec((None, d_model, DIL_BLOCK), lambda s, i: (prev_tile(s, i), 0, per - 1)),
            pl.BlockSpec((None, d_model, tt), lambda s, i: (tile(s, i), 0, 0)),
            pl.BlockSpec((DIL_BLOCK, d_model), lambda s, i: ((prev_tile(s, i) + 1) * per - 1, 0)),
            pl.BlockSpec((tt, d_model), lambda s, i: (tile(s, i), 0)),
        ],
        out_specs=[
            pl.BlockSpec((tt, d_model), lambda s, i: (tile(s, i), 0)),
            pl.BlockSpec((tt, 2 * HEAD_DIM), lambda s, i: (tile(s, i), 0)),
        ],
        out_shape=[
            jax.ShapeDtypeStruct((tokens, d_model), BF16),
            jax.ShapeDtypeStruct((tokens, 2 * HEAD_DIM), F32),
        ],
        compiler_params=_params(("arbitrary", "arbitrary")),
        name=name,
    )(q, kt, kt, v, v)


def _dil_out_kernel(*refs, dils, bounded):
    n = len(dils)
    o_refs, stat_refs = refs[:n], refs[n:2 * n]
    e_ref, w_ref, x_ref, g_ref, w1_ref, w2_ref, y_ref, o_slab, stat_slab = refs[2 * n:]
    tt, d_model = x_ref.shape
    n_slabs = d_model // LANES

    def natural(ref, dil, slab):
        if dil == 1:
            return ref[...].astype(F32)
        rows, width = tt // dil, ref.shape[-1]
        for r in range(dil):
            val = ref[r].astype(F32)
            for j in range(width // LANES):
                slab[j, pl.ds(r, rows, stride=dil), :] = val[:, j * LANES:(j + 1) * LANES]
        return jnp.concatenate([slab[j] for j in range(width // LANES)], axis=1)

    stats = [natural(ref, dil, stat_slab) for ref, dil in zip(stat_refs, dils)]
    outs = lambda g: natural(o_refs[g], dils[g], o_slab)

    def per_head_to_cols(w):
        hi = w.astype(BF16)
        lo = (w - hi.astype(F32)).astype(BF16)
        return jnp.dot(jnp.concatenate([hi, lo], axis=1), e_ref[...], preferred_element_type=F32)

    if bounded:
        lane = lax.broadcasted_iota(jnp.int32, stats[0].shape, 1)
        den = jnp.where(lane < d_model // HEAD_DIM, sum(stats[1:], stats[0]), 1.0)
        num = outs(0)
        for g in range(1, n):
            num = num + outs(g)
        merged = num * per_head_to_cols(1.0 / den)
    else:
        top = functools.reduce(jnp.maximum, stats)
        es = [jnp.exp2(l - top) for l in stats]
        den = sum(es[1:], es[0])
        merged = None
        for g in range(n):
            term = per_head_to_cols(es[g] / den) * outs(g)
            merged = term if merged is None else merged + term
    x1 = x_ref[...] + jnp.dot(merged.astype(BF16), w_ref[...], preferred_element_type=F32)
    y_ref[...] = _mlp_residual(x1, g_ref, w1_ref, w2_ref)


def _dil_out_mlp(outs, stats, expand, w, x2, g, w1, w2, *, batch, seq, tt, dils, bounded):
    tokens, d_model = x2.shape
    tpb = seq // tt

    def view_and_spec(a, dil):
        width = a.shape[-1]
        if dil == 1:
            return a, pl.BlockSpec((tt, width), lambda i: (i, 0))
        spec = pl.BlockSpec((None, dil, tt // dil, width), lambda i: (i // tpb, 0, i % tpb, 0))
        return a.reshape(batch, dil, seq // dil, width), spec

    args, specs = [], []
    for group in (outs, stats):
        for a, dil in zip(group, dils):
            a, spec = view_and_spec(a, dil)
            args.append(a)
            specs.append(spec)
    row = lambda i: (i, 0)
    return pl.pallas_call(
        functools.partial(_dil_out_kernel, dils=dils, bounded=bounded),
        grid=(tokens // tt,),
        in_specs=specs + [
            _resident((4 * HEAD_DIM, d_model)),
            _resident((d_model, d_model)),
            pl.BlockSpec((tt, d_model), row),
        ] + _mlp_specs(d_model, w1.shape[1]),
        out_specs=pl.BlockSpec((tt, d_model), row),
        out_shape=jax.ShapeDtypeStruct((tokens, d_model), F32),
        scratch_shapes=[pltpu.VMEM((d_model // LANES, tt, LANES), F32), pltpu.VMEM((1, tt, LANES), F32)],
        compiler_params=_params(("arbitrary",)),
        name="dil_outproj_mlp" + ("" if bounded else "_exact"),
    )(*args, expand, w, x2, g, w1, w2)


def _col(v):
    return v.reshape(-1, 1).astype(F32)


def _fox_layer(x2, norm_g, w_in, b_f, q_gain, k_gain, w_out, mlp, *, batch, seq, tt, tk, unroll):
    d_model = x2.shape[1]
    n_heads = d_model // HEAD_DIM
    tpb = seq // tt
    wt = w_in[:, :3 * d_model].T.astype(BF16)
    wf = w_in[:, 3 * d_model:].T.astype(BF16)
    qkv, c = _fox_inproj(x2, norm_g.reshape(1, -1), wt, wf, _col(b_f), _col(q_gain), _col(k_gain),
                         seq=seq, tt=tt)
    c4 = c.reshape(batch * tpb, n_heads, 1, tt)

    slack = 1.0 + 2.0 ** -7
    qb = math.sqrt(HEAD_DIM) * jnp.max(jnp.abs(q_gain)) * (LOG2E / math.sqrt(HEAD_DIM)) * slack
    kb = math.sqrt(HEAD_DIM) * jnp.max(jnp.abs(k_gain)) * slack
    per = tt // tk
    ch = c.reshape(batch, tpb, n_heads, tt).transpose(0, 2, 1, 3).reshape(batch, n_heads, seq)
    gap = ch[:, :, ::tt, None] - ch[:, :, None, tk - 1::tk]
    below = jnp.arange(seq // tk)[None, :] < (jnp.arange(tpb) * per)[:, None]
    jlo = jnp.sum((gap < -FOX_SKIP_LOG2) & below, axis=-1, dtype=jnp.int32).reshape(-1)

    ot = lax.cond(
        2.0 * qb * kb <= FOX_MAX_LOOSENESS_LOG2,
        lambda: _fox_attn_fast(qkv, c4, jlo, kb.reshape(1).astype(F32), batch=batch, seq=seq,
                               tk=tk, unroll=unroll),
        lambda: _fox_attn_exact(qkv, c4, batch=batch, seq=seq),
    )
    return _fox_out_mlp(ot, w_out.astype(BF16), x2, *mlp)


def _dil_layer(x2, positions, norm_g, w_in, q_gain, k_gain, w_out, mlp, *, batch, seq, tt):
    d_model = x2.shape[1]
    n_heads = d_model // HEAD_DIM
    dils = tuple(dil for _, dil in DIL_GROUPS)
    assert all(window // dil == DIL_BLOCK for window, dil in DIL_GROUPS)
    inv = (ROPE_THETA ** (-np.arange(ROT_DIM // 2, dtype=np.float64) * 2.0 / ROT_DIM)).astype(np.float32)
    inv = jnp.asarray(inv).reshape(-1, 1)
    hs = _dil_prep(x2, norm_g.reshape(1, -1), batch=batch, seq=seq, tt=tt, dils=dils)
    qkvs = []
    for g, dil in enumerate(dils):
        base = g * 3 * d_model
        wqk = w_in[:, base:base + 2 * d_model].T.astype(BF16)
        wv = w_in[:, base + 2 * d_model:base + 3 * d_model].astype(BF16)
        pos_perm = positions.reshape(batch, seq // dil, dil).transpose(0, 2, 1).reshape(1, -1)
        qkvs.append(_dil_inproj(hs[g].reshape(batch * seq, d_model), pos_perm, inv, wqk, wv,
                                _col(q_gain[g]), _col(k_gain[g]), tt=tt, name=f"dil_inproj_d{dil}"))
    head_of_col = np.arange(d_model) // HEAD_DIM
    expand = (np.arange(2 * HEAD_DIM)[:, None] == head_of_col[None, :]).astype(np.float32)
    expand = jnp.asarray(np.concatenate([expand, expand], axis=0), BF16)
    assert n_heads <= 2 * HEAD_DIM
    w_out = w_out.astype(BF16)

    def attend_and_merge(bounded):
        outs, stats = [], []
        for (q, kt, v), dil in zip(qkvs, dils):
            o, stat = _dil_attn(q, kt, v, n_seqs=batch * dil, tt=tt, bounded=bounded,
                                name=f"dil_attn_d{dil}" + ("" if bounded else "_exact"))
            outs.append(o)
            stats.append(stat)
        return _dil_out_mlp(outs, stats, expand, w_out, x2, *mlp, batch=batch, seq=seq, tt=tt, dils=dils,
                            bounded=bounded)

    bound = (math.sqrt(HEAD_DIM) * LOG2E * (1.0 + 2.0 ** -6)
             * jnp.max(jnp.max(jnp.abs(q_gain), axis=1) * jnp.max(jnp.abs(k_gain), axis=1)))
    return lax.cond(bound <= DIL_MAX_SCORE_LOG2, functools.partial(attend_and_merge, True),
                    functools.partial(attend_and_merge, False))


def _forward(x, positions, fox_norm, fox_w_in, fox_b_f, fox_q_gain, fox_k_gain, fox_w_out,
             dil_norm, dil_w_in, dil_q_gain, dil_k_gain, dil_w_out, mlp_norm, mlp_w1, mlp_w2,
             *, tt, tk_fox, unroll_fox):
    batch, seq, d_model = x.shape
    assert seq % (tt * max(dil for _, dil in DIL_GROUPS)) == 0 and tt % tk_fox == 0
    x2 = x.reshape(batch * seq, d_model)
    depth = mlp_norm.shape[0]
    for layer in range(depth):
        j = layer // 2
        mlp = (mlp_norm[layer].reshape(1, -1), mlp_w1[layer].astype(BF16), mlp_w2[layer].astype(BF16))
        if layer % 2 == 0:
            x2 = _fox_layer(x2, fox_norm[j], fox_w_in[j], fox_b_f[j], fox_q_gain[j], fox_k_gain[j],
                            fox_w_out[j], mlp, batch=batch, seq=seq, tt=tt, tk=tk_fox, unroll=unroll_fox)
        else:
            x2 = _dil_layer(x2, positions, dil_norm[j], dil_w_in[j], dil_q_gain[j], dil_k_gain[j],
                            dil_w_out[j], mlp, batch=batch, seq=seq, tt=tt)
    return x2.reshape(batch, seq, d_model)


def kernel(x, positions, fox_norm, fox_w_in, fox_b_f, fox_q_gain, fox_k_gain, fox_w_out, dil_norm, dil_w_in, dil_q_gain, dil_k_gain, dil_w_out, mlp_norm, mlp_w1, mlp_w2):
    return _forward(x, positions, fox_norm, fox_w_in, fox_b_f, fox_q_gain, fox_k_gain, fox_w_out,
                    dil_norm, dil_w_in, dil_q_gain, dil_k_gain, dil_w_out, mlp_norm, mlp_w1, mlp_w2,
                    tt=512, tk_fox=256, unroll_fox=6)
```

```python
import functools
import math

import jax
import jax.numpy as jnp
import numpy as np
from jax import lax
from jax.experimental import pallas as pl
from jax.experimental.pallas import tpu as pltpu

F32, BF16 = jnp.float32, jnp.bfloat16
HEAD_DIM = 64
ROT_DIM = 16
ROPE_THETA = 500000.0
DIL_GROUPS = ((128, 1), (512, 4), (2048, 16))
DIL_BLOCK = 128
EPS = 1e-6
NEG_INF = -1e30
LOG2E = 1.4426950408889634
LANES = 128
BIAS_ROWS = 3
FOX_SKIP_LOG2 = 160.0
FOX_MAX_LOOSENESS_LOG2 = 100.0
DIL_MAX_SCORE_LOG2 = 50.0
VMEM_LIMIT = 56 * 1024 * 1024

_NT = (((1,), (1,)), ((), ()))
_TN = (((0,), (0,)), ((), ()))


def _params(sem):
    return pltpu.CompilerParams(dimension_semantics=sem, vmem_limit_bytes=VMEM_LIMIT)


def _rms(x, g):
    return x * lax.rsqrt(jnp.mean(x * x, axis=-1, keepdims=True) + EPS) * g


def _head_norm(y, gain):
    inv = lax.rsqrt(jnp.mean(y * y, axis=1, keepdims=True) + EPS)
    return y * inv * gain[None]


def _split3(x):
    hi = x.astype(BF16).astype(F32)
    rem = x - hi
    mid = rem.astype(BF16).astype(F32)
    return hi, mid, rem - mid


def _fox_inproj_kernel(x_ref, g_ref, wt_ref, wf_ref, bf_ref, gq_ref, gk_ref,
                       qkv_ref, c_ref, carry_ref, *, tiles_per_batch):
    i = pl.program_id(0)

    @pl.when(i % tiles_per_batch == 0)
    def _():
        carry_ref[...] = jnp.zeros_like(carry_ref)

    h = _rms(x_ref[...], g_ref[...]).astype(BF16)
    tt, d_model = x_ref.shape
    n_heads = d_model // HEAD_DIM

    f = lax.dot_general(wf_ref[...], h, _NT, preferred_element_type=F32) + bf_ref[...]
    lf = (jnp.minimum(f, 0.0) - jnp.log1p(jnp.exp(-jnp.abs(f)))) * LOG2E
    lane = lax.broadcasted_iota(jnp.int32, lf.shape, 1)
    shift = 1
    while shift < tt:
        lf = lf + jnp.where(lane >= shift, pltpu.roll(lf, shift, 1), 0.0)
        shift *= 2
    c = lf + carry_ref[:, 0:1]
    c_ref[...] = c
    carry_ref[...] = jnp.broadcast_to(c[:, tt - 1:tt], carry_ref.shape)

    for part, gain_ref, mult in ((0, gq_ref, LOG2E / math.sqrt(HEAD_DIM)), (1, gk_ref, 1.0)):
        rows = slice(part * d_model, (part + 1) * d_model)
        y = lax.dot_general(wt_ref[rows, :], h, _NT, preferred_element_type=F32)
        y = _head_norm(y.reshape(n_heads, HEAD_DIM, tt), gain_ref[...] * mult)
        qkv_ref[rows, :] = y.reshape(d_model, tt).astype(BF16)
    rows = slice(2 * d_model, 3 * d_model)
    qkv_ref[rows, :] = lax.dot_general(wt_ref[rows, :], h, _NT, preferred_element_type=F32).astype(BF16)


def _fox_inproj(x2, g, wt, wf, bf, gq, gk, *, seq, tt):
    tokens, d_model = x2.shape
    n_heads = d_model // HEAD_DIM
    nt = tokens // tt
    const = lambda i: (0, 0)
    return pl.pallas_call(
        functools.partial(_fox_inproj_kernel, tiles_per_batch=seq // tt),
        grid=(nt,),
        in_specs=[
            pl.BlockSpec((tt, d_model), lambda i: (i, 0)),
            pl.BlockSpec((1, d_model), const),
            pl.BlockSpec((3 * d_model, d_model), const),
            pl.BlockSpec((n_heads, d_model), const),
            pl.BlockSpec((n_heads, 1), const),
            pl.BlockSpec((HEAD_DIM, 1), const),
            pl.BlockSpec((HEAD_DIM, 1), const),
        ],
        out_specs=[
            pl.BlockSpec((None, 3 * d_model, tt), lambda i: (i, 0, 0)),
            pl.BlockSpec((None, n_heads, tt), lambda i: (i, 0, 0)),
        ],
        out_shape=[
            jax.ShapeDtypeStruct((nt, 3 * d_model, tt), BF16),
            jax.ShapeDtypeStruct((nt, n_heads, tt), F32),
        ],
        scratch_shapes=[pltpu.VMEM((n_heads, LANES), F32)],
        compiler_params=_params(("arbitrary",)),
        name="fox_inproj",
    )(x2, g, wt, wf, bf, gq, gk)


def _build_keys(k_ref, c_ref, kp_ref, extra_rows):
    tiles, _, tt = k_ref.shape
    pad = jnp.zeros((HEAD_DIM - BIAS_ROWS - extra_rows, tt), F32)
    ones = [jnp.ones((extra_rows, tt), F32)] if extra_rows else []
    for n in range(tiles):
        blk = jnp.concatenate([k_ref[n].astype(F32), *_split3(-c_ref[n]), *ones, pad], axis=0)
        kp_ref[n * tt:(n + 1) * tt, :] = blk.T.astype(BF16)


def _fox_exact_kernel(q_ref, k_ref, v_ref, c_ref, o_ref, kp_ref):
    i = pl.program_id(2)
    tt = q_ref.shape[1]

    @pl.when(i == 0)
    def _():
        _build_keys(k_ref, c_ref, kp_ref, 0)

    ones_rows = (lax.broadcasted_iota(jnp.int32, (HEAD_DIM, tt), 0) < BIAS_ROWS).astype(BF16)
    qp = jnp.concatenate([q_ref[...], ones_rows], axis=0)

    def step(j, carry, masked):
        m, l, acc = carry
        off = pl.multiple_of(j * tt, tt)
        s = jnp.dot(kp_ref[pl.ds(off, tt), :], qp, preferred_element_type=F32)
        if masked:
            key = lax.broadcasted_iota(jnp.int32, s.shape, 0)
            qry = lax.broadcasted_iota(jnp.int32, s.shape, 1)
            s = jnp.where(key <= qry, s, NEG_INF)
        m_new = jnp.maximum(m, jnp.max(s, axis=0, keepdims=True))
        alpha = jnp.exp2(m - m_new)
        p = jnp.exp2(s - m_new)
        l = alpha * l + jnp.sum(p, axis=0, keepdims=True)
        pv = jnp.dot(v_ref[j], p.astype(BF16), preferred_element_type=F32)
        return m_new, l, alpha * acc + pv

    init = (jnp.full((1, tt), NEG_INF, F32), jnp.zeros((1, tt), F32), jnp.zeros((HEAD_DIM, tt), F32))
    carry = lax.fori_loop(0, i, lambda j, c: step(j, c, False), init)
    m, l, acc = step(i, carry, True)
    o_ref[...] = (acc / l).astype(BF16)


def _fox_fast_kernel(jlo_ref, kb_ref, q_ref, k_ref, v_ref, c_ref, o_ref, kp_ref, qp_ref, *sd_refs, tk, unroll):
    bh = pl.program_id(0) * pl.num_programs(1) + pl.program_id(1)
    nq, _, tq = q_ref.shape
    per, nkv = tq // tk, nq * (tq // tk)
    s_refs, d_refs = sd_refs[:per], sd_refs[per:]
    kb = kb_ref[0]
    pad_rows = HEAD_DIM - 2 * BIAS_ROWS

    _build_keys(k_ref, c_ref, kp_ref, BIAS_ROWS)
    lane = lax.broadcasted_iota(jnp.int32, (tk, 2 * HEAD_DIM), 1)
    kp_ref[nq * tq:nq * tq + tk, :] = jnp.where(lane == HEAD_DIM, NEG_INF, 0.0).astype(BF16)

    key = lax.broadcasted_iota(jnp.int32, (tk, tq), 0)
    qry = lax.broadcasted_iota(jnp.int32, (tk, tq), 1)

    def qk(blk, qp):
        off = pl.multiple_of(blk * tk, tk)
        return jnp.dot(kp_ref[pl.ds(off, tk), :], qp, preferred_element_type=F32)

    def prologue(i):
        qt = q_ref[i].astype(F32)
        qn = jnp.sqrt(jnp.sum(qt * qt, axis=0, keepdims=True))
        neg_m = c_ref[i] - qn * kb
        qp = jnp.concatenate([qt, jnp.ones((BIAS_ROWS, tq), F32), *_split3(neg_m),
                              jnp.zeros((pad_rows, tq), F32)], axis=0).astype(BF16)
        qp_ref[...] = qp
        for t in range(per):
            d_refs[t][...] = jnp.where(key + t * tk <= qry, qk(i * per + t, qp), NEG_INF)

    prologue(0)

    def q_block(i, _):
        qp = qp_ref[...]
        for t in range(per):
            s_refs[t][...] = d_refs[t][...]
        jlo = jlo_ref[bh * nq + i]
        n_off = i * per - jlo

        def pv(blk, s, carry):
            acc, l = carry
            off = pl.multiple_of(lax.rem(blk, per) * tk, tk)
            p = jnp.exp2(s)
            l = l + jnp.sum(p.reshape(tk // 8, 8, tq), axis=0)
            v_tile = v_ref[lax.div(blk, per), :, pl.ds(off, tk)]
            return acc + jnp.dot(v_tile, p.astype(BF16), preferred_element_type=F32), l

        def off_tile(g):
            return jnp.where(g < n_off, jlo + g, nkv), jnp.where(g < n_off, jlo + g, 0)

        def consumed_v(g):
            return jnp.where(g < per, i * per + g, off_tile(g - per)[1])

        def body(width, base, u, carry):
            pend = [s_refs[t][...] for t in range(per)]
            for t in range(width):
                g = base + u * width + t
                pend.append(qk(off_tile(g)[0], qp))
                carry = pv(consumed_v(g), pend.pop(0), carry)
            for t in range(per):
                s_refs[t][...] = pend[t]
            return carry

        n_main = n_off // unroll
        done = n_main * unroll
        n_tail = (n_off - done + 1) // 2
        carry = (jnp.zeros((HEAD_DIM, tq), F32), jnp.zeros((8, tq), F32))
        carry = lax.fori_loop(0, n_main, functools.partial(body, unroll, 0), carry)
        carry = lax.fori_loop(0, n_tail, functools.partial(body, 2, done), carry)
        prologue(jnp.minimum(i + 1, nq - 1))
        for t in range(per):
            carry = pv(consumed_v(done + 2 * n_tail + t), s_refs[t][...], carry)
        acc, l = carry
        o_ref[i] = (acc / jnp.sum(l, axis=0, keepdims=True)).astype(BF16)
        return 0

    lax.fori_loop(0, nq, q_block, 0)


def _fox_attn_fast(qkv, c4, jlo, kb, *, batch, seq, tk, unroll):
    nt, rows, tt = qkv.shape
    d_model = rows // 3
    n_heads = d_model // HEAD_DIM
    tpb = seq // tt
    grid_spec = pltpu.PrefetchScalarGridSpec(
        num_scalar_prefetch=1,
        grid=(batch, n_heads),
        in_specs=[
            pl.BlockSpec(memory_space=pltpu.SMEM),
            pl.BlockSpec((tpb, HEAD_DIM, tt), lambda b, h, jlo: (b, h, 0)),
            pl.BlockSpec((tpb, HEAD_DIM, tt), lambda b, h, jlo: (b, n_heads + h, 0)),
            pl.BlockSpec((tpb, HEAD_DIM, tt), lambda b, h, jlo: (b, 2 * n_heads + h, 0)),
            pl.BlockSpec((tpb, None, 1, tt), lambda b, h, jlo: (b, h, 0, 0)),
        ],
        out_specs=pl.BlockSpec((tpb, HEAD_DIM, tt), lambda b, h, jlo: (b, h, 0)),
        scratch_shapes=[pltpu.VMEM((seq + tk, 2 * HEAD_DIM), BF16), pltpu.VMEM((2 * HEAD_DIM, tt), BF16)]
        + [pltpu.VMEM((tk, tt), F32)] * (2 * (tt // tk)),
    )
    return pl.pallas_call(
        functools.partial(_fox_fast_kernel, tk=tk, unroll=unroll),
        grid_spec=grid_spec,
        out_shape=jax.ShapeDtypeStruct((nt, d_model, tt), BF16),
        compiler_params=_params(("arbitrary", "arbitrary")),
        name="fox_attn_fast",
    )(jlo, kb, qkv, qkv, qkv, c4)


def _fox_attn_exact(qkv, c4, *, batch, seq):
    nt, rows, tt = qkv.shape
    d_model = rows // 3
    n_heads = d_model // HEAD_DIM
    tpb = seq // tt
    return pl.pallas_call(
        _fox_exact_kernel,
        grid=(batch, n_heads, tpb),
        in_specs=[
            pl.BlockSpec((None, HEAD_DIM, tt), lambda b, h, i: (b * tpb + i, h, 0)),
            pl.BlockSpec((tpb, HEAD_DIM, tt), lambda b, h, i: (b, n_heads + h, 0)),
            pl.BlockSpec((tpb, HEAD_DIM, tt), lambda b, h, i: (b, 2 * n_heads + h, 0)),
            pl.BlockSpec((tpb, None, 1, tt), lambda b, h, i: (b, h, 0, 0)),
        ],
        out_specs=pl.BlockSpec((None, HEAD_DIM, tt), lambda b, h, i: (b * tpb + i, h, 0)),
        out_shape=jax.ShapeDtypeStruct((nt, d_model, tt), BF16),
        scratch_shapes=[pltpu.VMEM((seq, 2 * HEAD_DIM), BF16)],
        compiler_params=_params(("arbitrary", "arbitrary", "arbitrary")),
        name="fox_attn_exact",
    )(qkv, qkv, qkv, c4)


def _resident(shape):
    return pl.BlockSpec(shape, lambda i: (0,) * len(shape), pipeline_mode=pl.Buffered(1))


def _mlp_residual(x1, g_ref, w1_ref, w2_ref):
    h = _rms(x1, g_ref[...]).astype(BF16)
    a = jnp.dot(h, w1_ref[...], preferred_element_type=F32)
    a = jnp.square(jnp.maximum(a, 0.0)).astype(BF16)
    return x1 + jnp.dot(a, w2_ref[...], preferred_element_type=F32)


def _mlp_specs(d_model, d_ff):
    return [_resident((1, d_model)), _resident((d_model, d_ff)), _resident((d_ff, d_model))]


def _fox_out_mlp_kernel(ot_ref, w_ref, x_ref, g_ref, w1_ref, w2_ref, y_ref):
    x1 = x_ref[...] + lax.dot_general(ot_ref[...], w_ref[...], _TN, preferred_element_type=F32)
    y_ref[...] = _mlp_residual(x1, g_ref, w1_ref, w2_ref)


def _fox_out_mlp(ot, w, x2, g, w1, w2):
    nt, d_model, tt = ot.shape
    return pl.pallas_call(
        _fox_out_mlp_kernel,
        grid=(nt,),
        in_specs=[
            pl.BlockSpec((None, d_model, tt), lambda i: (i, 0, 0)),
            _resident((d_model, d_model)),
            pl.BlockSpec((tt, d_model), lambda i: (i, 0)),
        ] + _mlp_specs(d_model, w1.shape[1]),
        out_specs=pl.BlockSpec((tt, d_model), lambda i: (i, 0)),
        out_shape=jax.ShapeDtypeStruct(x2.shape, F32),
        compiler_params=_params(("arbitrary",)),
        name="fox_outproj_mlp",
    )(ot, w, x2, g, w1, w2)


def _dil_prep_kernel(x_ref, g_ref, *refs, dils):
    out_refs, slab_ref = refs[:-1], refs[-1]
    h = _rms(x_ref[...], g_ref[...])
    tt, d_model = h.shape
    n_slabs = d_model // LANES
    if any(d > 1 for d in dils):
        for j in range(n_slabs):
            slab_ref[j] = h[:, j * LANES:(j + 1) * LANES]
    for dil, o_ref in zip(dils, out_refs):
        if dil == 1:
            o_ref[...] = h.astype(BF16)
            continue
        n = tt // dil
        for r in range(dil):
            rows = [slab_ref[j, pl.ds(r, n, stride=dil), :] for j in range(n_slabs)]
            o_ref[r] = jnp.concatenate(rows, axis=1).astype(BF16)


def _dil_prep(x2, g, *, batch, seq, tt, dils):
    tokens, d_model = x2.shape
    tpb = seq // tt
    out_specs, out_shape = [], []
    for dil in dils:
        if dil == 1:
            out_specs.append(pl.BlockSpec((tt, d_model), lambda i: (i, 0)))
            out_shape.append(jax.ShapeDtypeStruct((tokens, d_model), BF16))
        else:
            out_specs.append(pl.BlockSpec((None, dil, tt // dil, d_model), lambda i: (i // tpb, 0, i % tpb, 0)))
            out_shape.append(jax.ShapeDtypeStruct((batch, dil, seq // dil, d_model), BF16))
    return pl.pallas_call(
        functools.partial(_dil_prep_kernel, dils=dils),
        grid=(tokens // tt,),
        in_specs=[pl.BlockSpec((tt, d_model), lambda i: (i, 0)), pl.BlockSpec((1, d_model), lambda i: (0, 0))],
        out_specs=out_specs,
        out_shape=out_shape,
        scratch_shapes=[pltpu.VMEM((d_model // LANES, tt, LANES), F32)],
        compiler_params=_params(("arbitrary",)),
        name="dil_prep",
    )(x2, g)


_TWO_PI = 2.0 * math.pi
_PI2_A = 6.28125
_PI2_B = float((np.array(_TWO_PI - _PI2_A, np.float32).view(np.uint32) & np.uint32(0xFFFFF000)).view(np.float32))
_PI2_C = _TWO_PI - _PI2_A - _PI2_B


def _rope_tables(pos_row, inv_col):
    ang = pos_row.astype(F32) * inv_col
    k = jnp.floor(ang * (1.0 / _TWO_PI) + 0.5)
    r = ((ang - k * _PI2_A) - k * _PI2_B) - k * _PI2_C
    return jnp.cos(r), jnp.sin(r)


def _dil_inproj_kernel(h_ref, pos_ref, inv_ref, wqk_ref, wv_ref, gq_ref, gk_ref, q_ref, kt_ref, v_ref):
    h = h_ref[...]
    tt, d_model = h.shape
    n_heads = d_model // HEAD_DIM
    half = ROT_DIM // 2
    cos, sin = _rope_tables(pos_ref[...], inv_ref[...])
    for part, gain_ref, mult in ((0, gq_ref, LOG2E / math.sqrt(HEAD_DIM)), (1, gk_ref, 1.0)):
        rows = slice(part * d_model, (part + 1) * d_model)
        y = lax.dot_general(wqk_ref[rows, :], h, _NT, preferred_element_type=F32)
        y = _head_norm(y.reshape(n_heads, HEAD_DIM, tt), gain_ref[...] * mult)
        x1, x2 = y[:, 0:half], y[:, half:ROT_DIM]
        y = jnp.concatenate([x1 * cos - x2 * sin, x2 * cos + x1 * sin, y[:, ROT_DIM:]], axis=1)
        y = y.reshape(d_model, tt)
        if part == 0:
            q_ref[...] = y.T.astype(BF16)
        else:
            kt_ref[...] = y.astype(BF16)
    v_ref[...] = jnp.dot(h, wv_ref[...], preferred_element_type=F32).astype(BF16)


def _dil_inproj(h2, pos_perm, inv, wqk, wv, gq, gk, *, tt, name):
    tokens, d_model = h2.shape
    nt = tokens // tt
    const = lambda i: (0, 0)
    return pl.pallas_call(
        _dil_inproj_kernel,
        grid=(nt,),
        in_specs=[
            pl.BlockSpec((tt, d_model), lambda i: (i, 0)),
            pl.BlockSpec((1, tt), lambda i: (0, i)),
            pl.BlockSpec((ROT_DIM // 2, 1), const),
            pl.BlockSpec((2 * d_model, d_model), const),
            pl.BlockSpec((d_model, d_model), const),
            pl.BlockSpec((HEAD_DIM, 1), const),
            pl.BlockSpec((HEAD_DIM, 1), const),
        ],
        out_specs=[
            pl.BlockSpec((tt, d_model), lambda i: (i, 0)),
            pl.BlockSpec((None, d_model, tt), lambda i: (i, 0, 0)),
            pl.BlockSpec((tt, d_model), lambda i: (i, 0)),
        ],
        out_shape=[
            jax.ShapeDtypeStruct((tokens, d_model), BF16),
            jax.ShapeDtypeStruct((nt, d_model, tt), BF16),
            jax.ShapeDtypeStruct((tokens, d_model), BF16),
        ],
        compiler_params=_params(("arbitrary",)),
        name=name,
    )(h2, pos_perm, inv, wqk, wv, gq, gk)


def _dil_attn_kernel(q_ref, ktp_ref, ktc_ref, vp_ref, vc_ref, o_ref, stat_ref, *, bounded):
    i = pl.program_id(1)
    blk = DIL_BLOCK
    has_prev = i > 0
    tq, d_model = q_ref.shape
    kt_all = jnp.concatenate([ktp_ref[...], ktc_ref[...]], axis=1)
    v_all = jnp.concatenate([vp_ref[...], vc_ref[...]], axis=0)
    r = lax.broadcasted_iota(jnp.int32, (blk, 2 * blk), 0)
    c = lax.broadcasted_iota(jnp.int32, (blk, 2 * blk), 1)
    band = (c >= r) & (c <= r + blk)
    first_mask = band & (has_prev | (c >= blk))
    lane = lax.broadcasted_iota(jnp.int32, (blk, 2 * HEAD_DIM), 1)
    low = lane < HEAD_DIM
    for u in range(tq // blk):
        rows = slice(u * blk, (u + 1) * blk)
        mask = first_mask if u == 0 else band
        stat_tile = jnp.zeros((blk, 2 * HEAD_DIM), F32)
        for pair in range(d_model // (2 * HEAD_DIM)):
            cols = slice(pair * 2 * HEAD_DIM, (pair + 1) * 2 * HEAD_DIM)
            qp = q_ref[rows, cols]
            kt = kt_all[cols, u * blk:(u + 2) * blk]
            vv = v_all[u * blk:(u + 2) * blk, cols]
            outs = []
            for hh, sel in ((0, low), (1, ~low)):
                qh = jnp.where(sel, qp, jnp.zeros_like(qp))
                s = jnp.dot(qh, kt, preferred_element_type=F32)
                s = jnp.where(mask, s, NEG_INF)
                if bounded:
                    p = jnp.exp2(s)
                    stat = jnp.sum(p, axis=1, keepdims=True)
                    outs.append(jnp.dot(p.astype(BF16), vv, preferred_element_type=F32))
                else:
                    m = jnp.max(s, axis=1, keepdims=True)
                    p = jnp.exp2(s - m)
                    den = jnp.sum(p, axis=1, keepdims=True)
                    stat = m + jnp.log2(den)
                    outs.append(jnp.dot(p.astype(BF16), vv, preferred_element_type=F32) * (1.0 / den))
                stat_tile = jnp.where(lane == 2 * pair + hh, stat, stat_tile)
            o_ref[rows, cols] = jnp.where(low, outs[0], outs[1]).astype(BF16)
        stat_ref[rows, :] = stat_tile


def _dil_attn(q, kt, v, *, n_seqs, tt, bounded, name):
    tokens, d_model = q.shape
    nq = tokens // tt // n_seqs
    per = tt // DIL_BLOCK
    tile = lambda s, i: s * nq + i
    prev_tile = lambda s, i: jnp.maximum(tile(s, i) - 1, 0)
    return pl.pallas_call(
        functools.partial(_dil_attn_kernel, bounded=bounded),
        grid=(n_seqs, nq),
        in_specs=[
            pl.BlockSpec((tt, d_model), lambda s, i: (tile(s, i), 0)),
            pl.BlockSpec((None, d_model, DIL_BLOCK), lambda s, i: (prev_tile(s, i), 0, per - 1)),
            pl.BlockSpec((None, d_model, tt), lambda s, i: (tile(s, i), 0, 0)),
            pl.BlockSpec((DIL_BLOCK, d_model), lambda s, i: ((prev_tile(s, i) + 1) * per - 1, 0)),
            pl.BlockSpec((tt, d_model), lambda s, i: (tile(s, i), 0)),
        ],
        out_specs=[
            pl.BlockSpec((tt, d_model), lambda s, i: (tile(s, i), 0)),
            pl.BlockSpec((tt, 2 * HEAD_DIM), lambda s, i: (tile(s, i), 0)),
        ],
        out_shape=[
            jax.ShapeDtypeStruct((tokens, d_model), BF16),
            jax.ShapeDtypeStruct((tokens, 2 * HEAD_DIM), F32),
        ],
        compiler_params=_params(("arbitrary", "arbitrary")),
        name=name,
    )(q, kt, kt, v, v)


def _dil_out_kernel(*refs, dils, bounded):
    n = len(dils)
    o_refs, stat_refs = refs[:n], refs[n:2 * n]
    e_ref, w_ref, x_ref, g_ref, w1_ref, w2_ref, y_ref, o_slab, stat_slab = refs[2 * n:]
    tt, d_model = x_ref.shape
    n_slabs = d_model // LANES

    def natural(ref, dil, slab):
        if dil == 1:
            return ref[...].astype(F32)
        rows, width = tt // dil, ref.shape[-1]
        for r in range(dil):
            val = ref[r].astype(F32)
            for j in range(width // LANES):
                slab[j, pl.ds(r, rows, stride=dil), :] = val[:, j * LANES:(j + 1) * LANES]
        return jnp.concatenate([slab[j] for j in range(width // LANES)], axis=1)

    stats = [natural(ref, dil, stat_slab) for ref, dil in zip(stat_refs, dils)]
    outs = lambda g: natural(o_refs[g], dils[g], o_slab)

    def per_head_to_cols(w):
        hi = w.astype(BF16)
        lo = (w - hi.astype(F32)).astype(BF16)
        return jnp.dot(jnp.concatenate([hi, lo], axis=1), e_ref[...], preferred_element_type=F32)

    if bounded:
        lane = lax.broadcasted_iota(jnp.int32, stats[0].shape, 1)
        den = jnp.where(lane < d_model // HEAD_DIM, sum(stats[1:], stats[0]), 1.0)
        num = outs(0)
        for g in range(1, n):
            num = num + outs(g)
        merged = num * per_head_to_cols(1.0 / den)
    else:
        top = functools.reduce(jnp.maximum, stats)
        es = [jnp.exp2(l - top) for l in stats]
        den = sum(es[1:], es[0])
        merged = None
        for g in range(n):
            term = per_head_to_cols(es[g] / den) * outs(g)
            merged = term if merged is None else merged + term
    x1 = x_ref[...] + jnp.dot(merged.astype(BF16), w_ref[...], preferred_element_type=F32)
    y_ref[...] = _mlp_residual(x1, g_ref, w1_ref, w2_ref)


def _dil_out_mlp(outs, stats, expand, w, x2, g, w1, w2, *, batch, seq, tt, dils, bounded):
    tokens, d_model = x2.shape
    tpb = seq // tt

    def view_and_spec(a, dil):
        width = a.shape[-1]
        if dil == 1:
            return a, pl.BlockSpec((tt, width), lambda i: (i, 0))
        spec = pl.BlockSpec((None, dil, tt // dil, width), lambda i: (i // tpb, 0, i % tpb, 0))
        return a.reshape(batch, dil, seq // dil, width), spec

    args, specs = [], []
    for group in (outs, stats):
        for a, dil in zip(group, dils):
            a, spec = view_and_spec(a, dil)
            args.append(a)
            specs.append(spec)
    row = lambda i: (i, 0)
    return pl.pallas_call(
        functools.partial(_dil_out_kernel, dils=dils, bounded=bounded),
        grid=(tokens // tt,),
        in_specs=specs + [
            _resident((4 * HEAD_DIM, d_model)),
            _resident((d_model, d_model)),
            pl.BlockSpec((tt, d_model), row),
        ] + _mlp_specs(d_model, w1.shape[1]),
        out_specs=pl.BlockSpec((tt, d_model), row),
        out_shape=jax.ShapeDtypeStruct((tokens, d_model), F32),
        scratch_shapes=[pltpu.VMEM((d_model // LANES, tt, LANES), F32), pltpu.VMEM((1, tt, LANES), F32)],
        compiler_params=_params(("arbitrary",)),
        name="dil_outproj_mlp" + ("" if bounded else "_exact"),
    )(*args, expand, w, x2, g, w1, w2)


def _col(v):
    return v.reshape(-1, 1).astype(F32)


def _fox_layer(x2, norm_g, w_in, b_f, q_gain, k_gain, w_out, mlp, *, batch, seq, tt, tk, unroll):
    d_model = x2.shape[1]
    n_heads = d_model // HEAD_DIM
    tpb = seq // tt
    wt = w_in[:, :3 * d_model].T.astype(BF16)
    wf = w_in[:, 3 * d_model:].T.astype(BF16)
    qkv, c = _fox_inproj(x2, norm_g.reshape(1, -1), wt, wf, _col(b_f), _col(q_gain), _col(k_gain),
                         seq=seq, tt=tt)
    c4 = c.reshape(batch * tpb, n_heads, 1, tt)

    slack = 1.0 + 2.0 ** -7
    qb = math.sqrt(HEAD_DIM) * jnp.max(jnp.abs(q_gain)) * (LOG2E / math.sqrt(HEAD_DIM)) * slack
    kb = math.sqrt(HEAD_DIM) * jnp.max(jnp.abs(k_gain)) * slack
    per = tt // tk
    ch = c.reshape(batch, tpb, n_heads, tt).transpose(0, 2, 1, 3).reshape(batch, n_heads, seq)
    gap = ch[:, :, ::tt, None] - ch[:, :, None, tk - 1::tk]
    below = jnp.arange(seq // tk)[None, :] < (jnp.arange(tpb) * per)[:, None]
    jlo = jnp.sum((gap < -FOX_SKIP_LOG2) & below, axis=-1, dtype=jnp.int32).reshape(-1)

    ot = lax.cond(
        2.0 * qb * kb <= FOX_MAX_LOOSENESS_LOG2,
        lambda: _fox_attn_fast(qkv, c4, jlo, kb.reshape(1).astype(F32), batch=batch, seq=seq,
                               tk=tk, unroll=unroll),
        lambda: _fox_attn_exact(qkv, c4, batch=batch, seq=seq),
    )
    return _fox_out_mlp(ot, w_out.astype(BF16), x2, *mlp)


def _dil_layer(x2, positions, norm_g, w_in, q_gain, k_gain, w_out, mlp, *, batch, seq, tt):
    d_model = x2.shape[1]
    n_heads = d_model // HEAD_DIM
    dils = tuple(dil for _, dil in DIL_GROUPS)
    assert all(window // dil == DIL_BLOCK for window, dil in DIL_GROUPS)
    inv = (ROPE_THETA ** (-np.arange(ROT_DIM // 2, dtype=np.float64) * 2.0 / ROT_DIM)).astype(np.float32)
    inv = jnp.asarray(inv).reshape(-1, 1)
    hs = _dil_prep(x2, norm_g.reshape(1, -1), batch=batch, seq=seq, tt=tt, dils=dils)
    qkvs = []
    for g, dil in enumerate(dils):
        base = g * 3 * d_model
        wqk = w_in[:, base:base + 2 * d_model].T.astype(BF16)
        wv = w_in[:, base + 2 * d_model:base + 3 * d_model].astype(BF16)
        pos_perm = positions.reshape(batch, seq // dil, dil).transpose(0, 2, 1).reshape(1, -1)
        qkvs.append(_dil_inproj(hs[g].reshape(batch * seq, d_model), pos_perm, inv, wqk, wv,
                                _col(q_gain[g]), _col(k_gain[g]), tt=tt, name=f"dil_inproj_d{dil}"))
    head_of_col = np.arange(d_model) // HEAD_DIM
    expand = (np.arange(2 * HEAD_DIM)[:, None] == head_of_col[None, :]).astype(np.float32)
    expand = jnp.asarray(np.concatenate([expand, expand], axis=0), BF16)
    assert n_heads <= 2 * HEAD_DIM
    w_out = w_out.astype(BF16)

    def attend_and_merge(bounded):
        outs, stats = [], []
        for (q, kt, v), dil in zip(qkvs, dils):
            o, stat = _dil_attn(q, kt, v, n_seqs=batch * dil, tt=tt, bounded=bounded,
                                name=f"dil_attn_d{dil}" + ("" if bounded else "_exact"))
            outs.append(o)
            stats.append(stat)
        return _dil_out_mlp(outs, stats, expand, w_out, x2, *mlp, batch=batch, seq=seq, tt=tt, dils=dils,
                            bounded=bounded)

    bound = (math.sqrt(HEAD_DIM) * LOG2E * (1.0 + 2.0 ** -6)
             * jnp.max(jnp.max(jnp.abs(q_gain), axis=1) * jnp.max(jnp.abs(k_gain), axis=1)))
    return lax.cond(bound <= DIL_MAX_SCORE_LOG2, functools.partial(attend_and_merge, True),
                    functools.partial(attend_and_merge, False))


def _forward(x, positions, fox_norm, fox_w_in, fox_b_f, fox_q_gain, fox_k_gain, fox_w_out,
             dil_norm, dil_w_in, dil_q_gain, dil_k_gain, dil_w_out, mlp_norm, mlp_w1, mlp_w2,
             *, tt, tk_fox, unroll_fox):
    batch, seq, d_model = x.shape
    assert seq % (tt * max(dil for _, dil in DIL_GROUPS)) == 0 and tt % tk_fox == 0
    x2 = x.reshape(batch * seq, d_model)
    depth = mlp_norm.shape[0]
    for layer in range(depth):
        j = layer // 2
        mlp = (mlp_norm[layer].reshape(1, -1), mlp_w1[layer].astype(BF16), mlp_w2[layer].astype(BF16))
        if layer % 2 == 0:
            x2 = _fox_layer(x2, fox_norm[j], fox_w_in[j], fox_b_f[j], fox_q_gain[j], fox_k_gain[j],
                            fox_w_out[j], mlp, batch=batch, seq=seq, tt=tt, tk=tk_fox, unroll=unroll_fox)
        else:
            x2 = _dil_layer(x2, positions, dil_norm[j], dil_w_in[j], dil_q_gain[j], dil_k_gain[j],
                            dil_w_out[j], mlp, batch=batch, seq=seq, tt=tt)
    return x2.reshape(batch, seq, d_model)


def kernel(x, positions, fox_norm, fox_w_in, fox_b_f, fox_q_gain, fox_k_gain, fox_w_out, dil_norm, dil_w_in, dil_q_gain, dil_k_gain, dil_w_out, mlp_norm, mlp_w1, mlp_w2):
    return _forward(x, positions, fox_norm, fox_w_in, fox_b_f, fox_q_gain, fox_k_gain, fox_w_out,
                    dil_norm, dil_w_in, dil_q_gain, dil_k_gain, dil_w_out, mlp_norm, mlp_w1, mlp_w2,
                    tt=512, tk_fox=256, unroll_fox=6)
```

```python
import functools
import math

import jax
import jax.numpy as jnp
import numpy as np
from jax import lax
from jax.experimental import pallas as pl
from jax.experimental.pallas import tpu as pltpu

F32, BF16 = jnp.float32, jnp.bfloat16
HEAD_DIM = 64
ROT_DIM = 16
ROPE_THETA = 500000.0
DIL_GROUPS = ((128, 1), (512, 4), (2048, 16))
DIL_BLOCK = 128
EPS = 1e-6
NEG_INF = -1e30
LOG2E = 1.4426950408889634
LANES = 128
BIAS_ROWS = 3
FOX_SKIP_LOG2 = 160.0
FOX_MAX_LOOSENESS_LOG2 = 100.0
DIL_MAX_SCORE_LOG2 = 50.0
VMEM_LIMIT = 56 * 1024 * 1024

_NT = (((1,), (1,)), ((), ()))
_TN = (((0,), (0,)), ((), ()))


def _params(sem):
    return pltpu.CompilerParams(dimension_semantics=sem, vmem_limit_bytes=VMEM_LIMIT)


def _rms(x, g):
    return x * lax.rsqrt(jnp.mean(x * x, axis=-1, keepdims=True) + EPS) * g


def _head_norm(y, gain):
    inv = lax.rsqrt(jnp.mean(y * y, axis=1, keepdims=True) + EPS)
    return y * inv * gain[None]


def _split3(x):
    hi = x.astype(BF16).astype(F32)
    rem = x - hi
    mid = rem.astype(BF16).astype(F32)
    return hi, mid, rem - mid


def _fox_inproj_kernel(x_ref, g_ref, wt_ref, wf_ref, bf_ref, gq_ref, gk_ref,
                       qkv_ref, c_ref, carry_ref, *, tiles_per_batch):
    i = pl.program_id(0)

    @pl.when(i % tiles_per_batch == 0)
    def _():
        carry_ref[...] = jnp.zeros_like(carry_ref)

    h = _rms(x_ref[...], g_ref[...]).astype(BF16)
    tt, d_model = x_ref.shape
    n_heads = d_model // HEAD_DIM

    f = lax.dot_general(wf_ref[...], h, _NT, preferred_element_type=F32) + bf_ref[...]
    lf = (jnp.minimum(f, 0.0) - jnp.log1p(jnp.exp(-jnp.abs(f)))) * LOG2E
    lane = lax.broadcasted_iota(jnp.int32, lf.shape, 1)
    shift = 1
    while shift < tt:
        lf = lf + jnp.where(lane >= shift, pltpu.roll(lf, shift, 1), 0.0)
        shift *= 2
    c = lf + carry_ref[:, 0:1]
    c_ref[...] = c
    carry_ref[...] = jnp.broadcast_to(c[:, tt - 1:tt], carry_ref.shape)

    for part, gain_ref, mult in ((0, gq_ref, LOG2E / math.sqrt(HEAD_DIM)), (1, gk_ref, 1.0)):
        rows = slice(part * d_model, (part + 1) * d_model)
        y = lax.dot_general(wt_ref[rows, :], h, _NT, preferred_element_type=F32)
        y = _head_norm(y.reshape(n_heads, HEAD_DIM, tt), gain_ref[...] * mult)
        qkv_ref[rows, :] = y.reshape(d_model, tt).astype(BF16)
    rows = slice(2 * d_model, 3 * d_model)
    qkv_ref[rows, :] = lax.dot_general(wt_ref[rows, :], h, _NT, preferred_element_type=F32).astype(BF16)


def _fox_inproj(x2, g, wt, wf, bf, gq, gk, *, seq, tt):
    tokens, d_model = x2.shape
    n_heads = d_model // HEAD_DIM
    nt = tokens // tt
    const = lambda i: (0, 0)
    return pl.pallas_call(
        functools.partial(_fox_inproj_kernel, tiles_per_batch=seq // tt),
        grid=(nt,),
        in_specs=[
            pl.BlockSpec((tt, d_model), lambda i: (i, 0)),
            pl.BlockSpec((1, d_model), const),
            pl.BlockSpec((3 * d_model, d_model), const),
            pl.BlockSpec((n_heads, d_model), const),
            pl.BlockSpec((n_heads, 1), const),
            pl.BlockSpec((HEAD_DIM, 1), const),
            pl.BlockSpec((HEAD_DIM, 1), const),
        ],
        out_specs=[
            pl.BlockSpec((None, 3 * d_model, tt), lambda i: (i, 0, 0)),
            pl.BlockSpec((None, n_heads, tt), lambda i: (i, 0, 0)),
        ],
        out_shape=[
            jax.ShapeDtypeStruct((nt, 3 * d_model, tt), BF16),
            jax.ShapeDtypeStruct((nt, n_heads, tt), F32),
        ],
        scratch_shapes=[pltpu.VMEM((n_heads, LANES), F32)],
        compiler_params=_params(("arbitrary",)),
        name="fox_inproj",
    )(x2, g, wt, wf, bf, gq, gk)


def _build_keys(k_ref, c_ref, kp_ref, extra_rows):
    tiles, _, tt = k_ref.shape
    pad = jnp.zeros((HEAD_DIM - BIAS_ROWS - extra_rows, tt), F32)
    ones = [jnp.ones((extra_rows, tt), F32)] if extra_rows else []
    for n in range(tiles):
        blk = jnp.concatenate([k_ref[n].astype(F32), *_split3(-c_ref[n]), *ones, pad], axis=0)
        kp_ref[n * tt:(n + 1) * tt, :] = blk.T.astype(BF16)


def _fox_exact_kernel(q_ref, k_ref, v_ref, c_ref, o_ref, kp_ref):
    i = pl.program_id(2)
    tt = q_ref.shape[1]

    @pl.when(i == 0)
    def _():
        _build_keys(k_ref, c_ref, kp_ref, 0)

    ones_rows = (lax.broadcasted_iota(jnp.int32, (HEAD_DIM, tt), 0) < BIAS_ROWS).astype(BF16)
    qp = jnp.concatenate([q_ref[...], ones_rows], axis=0)

    def step(j, carry, masked):
        m, l, acc = carry
        off = pl.multiple_of(j * tt, tt)
        s = jnp.dot(kp_ref[pl.ds(off, tt), :], qp, preferred_element_type=F32)
        if masked:
            key = lax.broadcasted_iota(jnp.int32, s.shape, 0)
            qry = lax.broadcasted_iota(jnp.int32, s.shape, 1)
            s = jnp.where(key <= qry, s, NEG_INF)
        m_new = jnp.maximum(m, jnp.max(s, axis=0, keepdims=True))
        alpha = jnp.exp2(m - m_new)
        p = jnp.exp2(s - m_new)
        l = alpha * l + jnp.sum(p, axis=0, keepdims=True)
        pv = jnp.dot(v_ref[j], p.astype(BF16), preferred_element_type=F32)
        return m_new, l, alpha * acc + pv

    init = (jnp.full((1, tt), NEG_INF, F32), jnp.zeros((1, tt), F32), jnp.zeros((HEAD_DIM, tt), F32))
    carry = lax.fori_loop(0, i, lambda j, c: step(j, c, False), init)
    m, l, acc = step(i, carry, True)
    o_ref[...] = (acc / l).astype(BF16)


def _fox_fast_kernel(jlo_ref, kb_ref, q_ref, k_ref, v_ref, c_ref, o_ref, kp_ref, qp_ref, *sd_refs, tk, unroll):
    bh = pl.program_id(0) * pl.num_programs(1) + pl.program_id(1)
    nq, _, tq = q_ref.shape
    per, nkv = tq // tk, nq * (tq // tk)
    s_refs, d_refs = sd_refs[:per], sd_refs[per:]
    kb = kb_ref[0]
    pad_rows = HEAD_DIM - 2 * BIAS_ROWS

    _build_keys(k_ref, c_ref, kp_ref, BIAS_ROWS)
    lane = lax.broadcasted_iota(jnp.int32, (tk, 2 * HEAD_DIM), 1)
    kp_ref[nq * tq:nq * tq + tk, :] = jnp.where(lane == HEAD_DIM, NEG_INF, 0.0).astype(BF16)

    key = lax.broadcasted_iota(jnp.int32, (tk, tq), 0)
    qry = lax.broadcasted_iota(jnp.int32, (tk, tq), 1)

    def qk(blk, qp):
        off = pl.multiple_of(blk * tk, tk)
        return jnp.dot(kp_ref[pl.ds(off, tk), :], qp, preferred_element_type=F32)

    def prologue(i):
        qt = q_ref[i].astype(F32)
        qn = jnp.sqrt(jnp.sum(qt * qt, axis=0, keepdims=True))
        neg_m = c_ref[i] - qn * kb
        qp = jnp.concatenate([qt, jnp.ones((BIAS_ROWS, tq), F32), *_split3(neg_m),
                              jnp.zeros((pad_rows, tq), F32)], axis=0).astype(BF16)
        qp_ref[...] = qp
        for t in range(per):
            d_refs[t][...] = jnp.where(key + t * tk <= qry, qk(i * per + t, qp), NEG_INF)

    prologue(0)

    def q_block(i, _):
        qp = qp_ref[...]
        for t in range(per):
            s_refs[t][...] = d_refs[t][...]
        jlo = jlo_ref[bh * nq + i]
        n_off = i * per - jlo

        def pv(blk, s, carry):
            acc, l = carry
            off = pl.multiple_of(lax.rem(blk, per) * tk, tk)
            p = jnp.exp2(s)
            l = l + jnp.sum(p.reshape(tk // 8, 8, tq), axis=0)
            v_tile = v_ref[lax.div(blk, per), :, pl.ds(off, tk)]
            return acc + jnp.dot(v_tile, p.astype(BF16), preferred_element_type=F32), l

        def off_tile(g):
            return jnp.where(g < n_off, jlo + g, nkv), jnp.where(g < n_off, jlo + g, 0)

        def consumed_v(g):
            return jnp.where(g < per, i * per + g, off_tile(g - per)[1])

        def body(width, base, u, carry):
            pend = [s_refs[t][...] for t in range(per)]
            for t in range(width):
                g = base + u * width + t
                pend.append(qk(off_tile(g)[0], qp))
                carry = pv(consumed_v(g), pend.pop(0), carry)
            for t in range(per):
                s_refs[t][...] = pend[t]
            return carry

        carry = (jnp.zeros((HEAD_DIM, tq), F32), jnp.zeros((8, tq), F32))
        done = 0
        for level, width in enumerate(unroll):
            pad = width - 1 if level == len(unroll) - 1 else 0
            trips = (n_off - done + pad) // width
            carry = lax.fori_loop(0, trips, functools.partial(body, width, done), carry)
            done = done + trips * width
        prologue(jnp.minimum(i + 1, nq - 1))
        for t in range(per):
            carry = pv(consumed_v(done + t), s_refs[t][...], carry)
        acc, l = carry
        o_ref[i] = (acc / jnp.sum(l, axis=0, keepdims=True)).astype(BF16)
        return 0

    lax.fori_loop(0, nq, q_block, 0)


def _fox_attn_fast(qkv, c4, jlo, kb, *, batch, seq, tk, unroll):
    nt, rows, tt = qkv.shape
    d_model = rows // 3
    n_heads = d_model // HEAD_DIM
    tpb = seq // tt
    grid_spec = pltpu.PrefetchScalarGridSpec(
        num_scalar_prefetch=1,
        grid=(batch, n_heads),
        in_specs=[
            pl.BlockSpec(memory_space=pltpu.SMEM),
            pl.BlockSpec((tpb, HEAD_DIM, tt), lambda b, h, jlo: (b, h, 0)),
            pl.BlockSpec((tpb, HEAD_DIM, tt), lambda b, h, jlo: (b, n_heads + h, 0)),
            pl.BlockSpec((tpb, HEAD_DIM, tt), lambda b, h, jlo: (b, 2 * n_heads + h, 0)),
            pl.BlockSpec((tpb, None, 1, tt), lambda b, h, jlo: (b, h, 0, 0)),
        ],
        out_specs=pl.BlockSpec((tpb, HEAD_DIM, tt), lambda b, h, jlo: (b, h, 0)),
        scratch_shapes=[pltpu.VMEM((seq + tk, 2 * HEAD_DIM), BF16), pltpu.VMEM((2 * HEAD_DIM, tt), BF16)]
        + [pltpu.VMEM((tk, tt), F32)] * (2 * (tt // tk)),
    )
    return pl.pallas_call(
        functools.partial(_fox_fast_kernel, tk=tk, unroll=unroll),
        grid_spec=grid_spec,
        out_shape=jax.ShapeDtypeStruct((nt, d_model, tt), BF16),
        compiler_params=_params(("arbitrary", "arbitrary")),
        name="fox_attn_fast",
    )(jlo, kb, qkv, qkv, qkv, c4)


def _fox_attn_exact(qkv, c4, *, batch, seq):
    nt, rows, tt = qkv.shape
    d_model = rows // 3
    n_heads = d_model // HEAD_DIM
    tpb = seq // tt
    return pl.pallas_call(
        _fox_exact_kernel,
        grid=(batch, n_heads, tpb),
        in_specs=[
            pl.BlockSpec((None, HEAD_DIM, tt), lambda b, h, i: (b * tpb + i, h, 0)),
            pl.BlockSpec((tpb, HEAD_DIM, tt), lambda b, h, i: (b, n_heads + h, 0)),
            pl.BlockSpec((tpb, HEAD_DIM, tt), lambda b, h, i: (b, 2 * n_heads + h, 0)),
            pl.BlockSpec((tpb, None, 1, tt), lambda b, h, i: (b, h, 0, 0)),
        ],
        out_specs=pl.BlockSpec((None, HEAD_DIM, tt), lambda b, h, i: (b * tpb + i, h, 0)),
        out_shape=jax.ShapeDtypeStruct((nt, d_model, tt), BF16),
        scratch_shapes=[pltpu.VMEM((seq, 2 * HEAD_DIM), BF16)],
        compiler_params=_params(("arbitrary", "arbitrary", "arbitrary")),
        name="fox_attn_exact",
    )(qkv, qkv, qkv, c4)


def _resident(shape):
    return pl.BlockSpec(shape, lambda i: (0,) * len(shape), pipeline_mode=pl.Buffered(1))


def _mlp_residual(x1, g_ref, w1_ref, w2_ref):
    h = _rms(x1, g_ref[...]).astype(BF16)
    a = jnp.dot(h, w1_ref[...], preferred_element_type=F32)
    a = jnp.square(jnp.maximum(a, 0.0)).astype(BF16)
    return x1 + jnp.dot(a, w2_ref[...], preferred_element_type=F32)


def _mlp_specs(d_model, d_ff):
    return [_resident((1, d_model)), _resident((d_model, d_ff)), _resident((d_ff, d_model))]


def _fox_out_mlp_kernel(ot_ref, w_ref, x_ref, g_ref, w1_ref, w2_ref, y_ref):
    x1 = x_ref[...] + lax.dot_general(ot_ref[...], w_ref[...], _TN, preferred_element_type=F32)
    y_ref[...] = _mlp_residual(x1, g_ref, w1_ref, w2_ref)


def _fox_out_mlp(ot, w, x2, g, w1, w2):
    nt, d_model, tt = ot.shape
    return pl.pallas_call(
        _fox_out_mlp_kernel,
        grid=(nt,),
        in_specs=[
            pl.BlockSpec((None, d_model, tt), lambda i: (i, 0, 0)),
            _resident((d_model, d_model)),
            pl.BlockSpec((tt, d_model), lambda i: (i, 0)),
        ] + _mlp_specs(d_model, w1.shape[1]),
        out_specs=pl.BlockSpec((tt, d_model), lambda i: (i, 0)),
        out_shape=jax.ShapeDtypeStruct(x2.shape, F32),
        compiler_params=_params(("arbitrary",)),
        name="fox_outproj_mlp",
    )(ot, w, x2, g, w1, w2)


def _dil_prep_kernel(x_ref, g_ref, *refs, dils):
    out_refs, slab_ref = refs[:-1], refs[-1]
    h = _rms(x_ref[...], g_ref[...])
    tt, d_model = h.shape
    n_slabs = d_model // LANES
    if any(d > 1 for d in dils):
        for j in range(n_slabs):
            slab_ref[j] = h[:, j * LANES:(j + 1) * LANES]
    for dil, o_ref in zip(dils, out_refs):
        if dil == 1:
            o_ref[...] = h.astype(BF16)
            continue
        n = tt // dil
        for r in range(dil):
            rows = [slab_ref[j, pl.ds(r, n, stride=dil), :] for j in range(n_slabs)]
            o_ref[r] = jnp.concatenate(rows, axis=1).astype(BF16)


def _dil_prep(x2, g, *, batch, seq, tt, dils):
    tokens, d_model = x2.shape
    tpb = seq // tt
    out_specs, out_shape = [], []
    for dil in dils:
        if dil == 1:
            out_specs.append(pl.BlockSpec((tt, d_model), lambda i: (i, 0)))
            out_shape.append(jax.ShapeDtypeStruct((tokens, d_model), BF16))
        else:
            out_specs.append(pl.BlockSpec((None, dil, tt // dil, d_model), lambda i: (i // tpb, 0, i % tpb, 0)))
            out_shape.append(jax.ShapeDtypeStruct((batch, dil, seq // dil, d_model), BF16))
    return pl.pallas_call(
        functools.partial(_dil_prep_kernel, dils=dils),
        grid=(tokens // tt,),
        in_specs=[pl.BlockSpec((tt, d_model), lambda i: (i, 0)), pl.BlockSpec((1, d_model), lambda i: (0, 0))],
        out_specs=out_specs,
        out_shape=out_shape,
        scratch_shapes=[pltpu.VMEM((d_model // LANES, tt, LANES), F32)],
        compiler_params=_params(("arbitrary",)),
        name="dil_prep",
    )(x2, g)


_TWO_PI = 2.0 * math.pi
_PI2_A = 6.28125
_PI2_B = float((np.array(_TWO_PI - _PI2_A, np.float32).view(np.uint32) & np.uint32(0xFFFFF000)).view(np.float32))
_PI2_C = _TWO_PI - _PI2_A - _PI2_B


def _rope_tables(pos_row, inv_col):
    ang = pos_row.astype(F32) * inv_col
    k = jnp.floor(ang * (1.0 / _TWO_PI) + 0.5)
    r = ((ang - k * _PI2_A) - k * _PI2_B) - k * _PI2_C
    return jnp.cos(r), jnp.sin(r)


def _dil_inproj_kernel(h_ref, pos_ref, inv_ref, wqk_ref, wv_ref, gq_ref, gk_ref, q_ref, kt_ref, v_ref):
    h = h_ref[...]
    tt, d_model = h.shape
    n_heads = d_model // HEAD_DIM
    half = ROT_DIM // 2
    cos, sin = _rope_tables(pos_ref[...], inv_ref[...])
    for part, gain_ref, mult in ((0, gq_ref, LOG2E / math.sqrt(HEAD_DIM)), (1, gk_ref, 1.0)):
        rows = slice(part * d_model, (part + 1) * d_model)
        y = lax.dot_general(wqk_ref[rows, :], h, _NT, preferred_element_type=F32)
        y = _head_norm(y.reshape(n_heads, HEAD_DIM, tt), gain_ref[...] * mult)
        x1, x2 = y[:, 0:half], y[:, half:ROT_DIM]
        y = jnp.concatenate([x1 * cos - x2 * sin, x2 * cos + x1 * sin, y[:, ROT_DIM:]], axis=1)
        y = y.reshape(d_model, tt)
        if part == 0:
            q_ref[...] = y.T.astype(BF16)
        else:
            kt_ref[...] = y.astype(BF16)
    v_ref[...] = jnp.dot(h, wv_ref[...], preferred_element_type=F32).astype(BF16)


def _dil_inproj(h2, pos_perm, inv, wqk, wv, gq, gk, *, tt, name):
    tokens, d_model = h2.shape
    nt = tokens // tt
    const = lambda i: (0, 0)
    return pl.pallas_call(
        _dil_inproj_kernel,
        grid=(nt,),
        in_specs=[
            pl.BlockSpec((tt, d_model), lambda i: (i, 0)),
            pl.BlockSpec((1, tt), lambda i: (0, i)),
            pl.BlockSpec((ROT_DIM // 2, 1), const),
            pl.BlockSpec((2 * d_model, d_model), const),
            pl.BlockSpec((d_model, d_model), const),
            pl.BlockSpec((HEAD_DIM, 1), const),
            pl.BlockSpec((HEAD_DIM, 1), const),
        ],
        out_specs=[
            pl.BlockSpec((tt, d_model), lambda i: (i, 0)),
            pl.BlockSpec((None, d_model, tt), lambda i: (i, 0, 0)),
            pl.BlockSpec((tt, d_model), lambda i: (i, 0)),
        ],
        out_shape=[
            jax.ShapeDtypeStruct((tokens, d_model), BF16),
            jax.ShapeDtypeStruct((nt, d_model, tt), BF16),
            jax.ShapeDtypeStruct((tokens, d_model), BF16),
        ],
        compiler_params=_params(("arbitrary",)),
        name=name,
    )(h2, pos_perm, inv, wqk, wv, gq, gk)


def _dil_attn_kernel(q_ref, ktp_ref, ktc_ref, vp_ref, vc_ref, o_ref, stat_ref, *, bounded):
    i = pl.program_id(1)
    blk = DIL_BLOCK
    has_prev = i > 0
    tq, d_model = q_ref.shape
    kt_all = jnp.concatenate([ktp_ref[...], ktc_ref[...]], axis=1)
    v_all = jnp.concatenate([vp_ref[...], vc_ref[...]], axis=0)
    r = lax.broadcasted_iota(jnp.int32, (blk, 2 * blk), 0)
    c = lax.broadcasted_iota(jnp.int32, (blk, 2 * blk), 1)
    band = (c >= r) & (c <= r + blk)
    first_mask = band & (has_prev | (c >= blk))
    lane = lax.broadcasted_iota(jnp.int32, (blk, 2 * HEAD_DIM), 1)
    low = lane < HEAD_DIM
    for u in range(tq // blk):
        rows = slice(u * blk, (u + 1) * blk)
        mask = first_mask if u == 0 else band
        stat_tile = jnp.zeros((blk, 2 * HEAD_DIM), F32)
        for pair in range(d_model // (2 * HEAD_DIM)):
            cols = slice(pair * 2 * HEAD_DIM, (pair + 1) * 2 * HEAD_DIM)
            qp = q_ref[rows, cols]
            kt = kt_all[cols, u * blk:(u + 2) * blk]
            vv = v_all[u * blk:(u + 2) * blk, cols]
            outs = []
            for hh, sel in ((0, low), (1, ~low)):
                qh = jnp.where(sel, qp, jnp.zeros_like(qp))
                s = jnp.dot(qh, kt, preferred_element_type=F32)
                s = jnp.where(mask, s, NEG_INF)
                if bounded:
                    p = jnp.exp2(s)
                    stat = jnp.sum(p, axis=1, keepdims=True)
                    outs.append(jnp.dot(p.astype(BF16), vv, preferred_element_type=F32))
                else:
                    m = jnp.max(s, axis=1, keepdims=True)
                    p = jnp.exp2(s - m)
                    den = jnp.sum(p, axis=1, keepdims=True)
                    stat = m + jnp.log2(den)
                    outs.append(jnp.dot(p.astype(BF16), vv, preferred_element_type=F32) * (1.0 / den))
                stat_tile = jnp.where(lane == 2 * pair + hh, stat, stat_tile)
            o_ref[rows, cols] = jnp.where(low, outs[0], outs[1]).astype(BF16)
        stat_ref[rows, :] = stat_tile


def _dil_attn(q, kt, v, *, n_seqs, tt, bounded, name):
    tokens, d_model = q.shape
    nq = tokens // tt // n_seqs
    per = tt // DIL_BLOCK
    tile = lambda s, i: s * nq + i
    prev_tile = lambda s, i: jnp.maximum(tile(s, i) - 1, 0)
    return pl.pallas_call(
        functools.partial(_dil_attn_kernel, bounded=bounded),
        grid=(n_seqs, nq),
        in_specs=[
            pl.BlockSpec((tt, d_model), lambda s, i: (tile(s, i), 0)),
            pl.BlockSpec((None, d_model, DIL_BLOCK), lambda s, i: (prev_tile(s, i), 0, per - 1)),
            pl.BlockSpec((None, d_model, tt), lambda s, i: (tile(s, i), 0, 0)),
            pl.BlockSpec((DIL_BLOCK, d_model), lambda s, i: ((prev_tile(s, i) + 1) * per - 1, 0)),
            pl.BlockSpec((tt, d_model), lambda s, i: (tile(s, i), 0)),
        ],
        out_specs=[
            pl.BlockSpec((tt, d_model), lambda s, i: (tile(s, i), 0)),
            pl.BlockSpec((tt, 2 * HEAD_DIM), lambda s, i: (tile(s, i), 0)),
        ],
        out_shape=[
            jax.ShapeDtypeStruct((tokens, d_model), BF16),
            jax.ShapeDtypeStruct((tokens, 2 * HEAD_DIM), F32),
        ],
        compiler_params=_params(("arbitrary", "arbitrary")),
        name=name,
    )(q, kt, kt, v, v)


def _dil_out_kernel(*refs, dils, bounded):
    n = len(dils)
    o_refs, stat_refs = refs[:n], refs[n:2 * n]
    e_ref, w_ref, x_ref, g_ref, w1_ref, w2_ref, y_ref, o_slab, stat_slab = refs[2 * n:]
    tt, d_model = x_ref.shape

    def natural(ref, dil, slab):
        if dil == 1:
            return ref[...].astype(F32)
        rows, width = tt // dil, ref.shape[-1]
        for r in range(dil):
            val = ref[r].astype(F32)
            for j in range(width // LANES):
                slab[j, pl.ds(r, rows, stride=dil), :] = val[:, j * LANES:(j + 1) * LANES]
        return jnp.concatenate([slab[j] for j in range(width // LANES)], axis=1)

    stats = [natural(ref, dil, stat_slab) for ref, dil in zip(stat_refs, dils)]
    outs = lambda g: natural(o_refs[g], dils[g], o_slab)

    def per_head_to_cols(w):
        hi = w.astype(BF16)
        lo = (w - hi.astype(F32)).astype(BF16)
        return jnp.dot(jnp.concatenate([hi, lo], axis=1), e_ref[...], preferred_element_type=F32)

    if bounded:
        lane = lax.broadcasted_iota(jnp.int32, stats[0].shape, 1)
        den = jnp.where(lane < d_model // HEAD_DIM, sum(stats[1:], stats[0]), 1.0)
        num = outs(0)
        for g in range(1, n):
            num = num + outs(g)
        merged = num * per_head_to_cols(1.0 / den)
    else:
        top = functools.reduce(jnp.maximum, stats)
        es = [jnp.exp2(l - top) for l in stats]
        den = sum(es[1:], es[0])
        merged = None
        for g in range(n):
            term = per_head_to_cols(es[g] / den) * outs(g)
            merged = term if merged is None else merged + term
    x1 = x_ref[...] + jnp.dot(merged.astype(BF16), w_ref[...], preferred_element_type=F32)
    y_ref[...] = _mlp_residual(x1, g_ref, w1_ref, w2_ref)


def _dil_out_mlp(outs, stats, expand, w, x2, g, w1, w2, *, batch, seq, tt, dils, bounded):
    tokens, d_model = x2.shape
    tpb = seq // tt

    def view_and_spec(a, dil):
        width = a.shape[-1]
        if dil == 1:
            return a, pl.BlockSpec((tt, width), lambda i: (i, 0))
        spec = pl.BlockSpec((None, dil, tt // dil, width), lambda i: (i // tpb, 0, i % tpb, 0))
        return a.reshape(batch, dil, seq // dil, width), spec

    args, specs = [], []
    for group in (outs, stats):
        for a, dil in zip(group, dils):
            a, spec = view_and_spec(a, dil)
            args.append(a)
            specs.append(spec)
    row = lambda i: (i, 0)
    return pl.pallas_call(
        functools.partial(_dil_out_kernel, dils=dils, bounded=bounded),
        grid=(tokens // tt,),
        in_specs=specs + [
            _resident((4 * HEAD_DIM, d_model)),
            _resident((d_model, d_model)),
            pl.BlockSpec((tt, d_model), row),
        ] + _mlp_specs(d_model, w1.shape[1]),
        out_specs=pl.BlockSpec((tt, d_model), row),
        out_shape=jax.ShapeDtypeStruct((tokens, d_model), F32),
        scratch_shapes=[pltpu.VMEM((d_model // LANES, tt, LANES), F32), pltpu.VMEM((1, tt, LANES), F32)],
        compiler_params=_params(("arbitrary",)),
        name="dil_outproj_mlp" + ("" if bounded else "_exact"),
    )(*args, expand, w, x2, g, w1, w2)


def _col(v):
    return v.reshape(-1, 1).astype(F32)


def _fox_layer(x2, norm_g, w_in, b_f, q_gain, k_gain, w_out, mlp, *, batch, seq, tt, tk, unroll):
    d_model = x2.shape[1]
    n_heads = d_model // HEAD_DIM
    tpb = seq // tt
    wt = w_in[:, :3 * d_model].T.astype(BF16)
    wf = w_in[:, 3 * d_model:].T.astype(BF16)
    qkv, c = _fox_inproj(x2, norm_g.reshape(1, -1), wt, wf, _col(b_f), _col(q_gain), _col(k_gain),
                         seq=seq, tt=tt)
    c4 = c.reshape(batch * tpb, n_heads, 1, tt)

    slack = 1.0 + 2.0 ** -7
    qb = math.sqrt(HEAD_DIM) * jnp.max(jnp.abs(q_gain)) * (LOG2E / math.sqrt(HEAD_DIM)) * slack
    kb = math.sqrt(HEAD_DIM) * jnp.max(jnp.abs(k_gain)) * slack
    per = tt // tk
    ch = c.reshape(batch, tpb, n_heads, tt).transpose(0, 2, 1, 3).reshape(batch, n_heads, seq)
    gap = ch[:, :, ::tt, None] - ch[:, :, None, tk - 1::tk]
    below = jnp.arange(seq // tk)[None, :] < (jnp.arange(tpb) * per)[:, None]
    jlo = jnp.sum((gap < -FOX_SKIP_LOG2) & below, axis=-1, dtype=jnp.int32).reshape(-1)

    ot = lax.cond(
        2.0 * qb * kb <= FOX_MAX_LOOSENESS_LOG2,
        lambda: _fox_attn_fast(qkv, c4, jlo, kb.reshape(1).astype(F32), batch=batch, seq=seq,
                               tk=tk, unroll=unroll),
        lambda: _fox_attn_exact(qkv, c4, batch=batch, seq=seq),
    )
    return _fox_out_mlp(ot, w_out.astype(BF16), x2, *mlp)


def _dil_layer(x2, positions, norm_g, w_in, q_gain, k_gain, w_out, mlp, *, batch, seq, tt):
    d_model = x2.shape[1]
    n_heads = d_model // HEAD_DIM
    dils = tuple(dil for _, dil in DIL_GROUPS)
    assert all(window // dil == DIL_BLOCK for window, dil in DIL_GROUPS)
    inv = (ROPE_THETA ** (-np.arange(ROT_DIM // 2, dtype=np.float64) * 2.0 / ROT_DIM)).astype(np.float32)
    inv = jnp.asarray(inv).reshape(-1, 1)
    hs = _dil_prep(x2, norm_g.reshape(1, -1), batch=batch, seq=seq, tt=tt, dils=dils)
    qkvs = []
    for g, dil in enumerate(dils):
        base = g * 3 * d_model
        wqk = w_in[:, base:base + 2 * d_model].T.astype(BF16)
        wv = w_in[:, base + 2 * d_model:base + 3 * d_model].astype(BF16)
        pos_perm = positions.reshape(batch, seq // dil, dil).transpose(0, 2, 1).reshape(1, -1)
        qkvs.append(_dil_inproj(hs[g].reshape(batch * seq, d_model), pos_perm, inv, wqk, wv,
                                _col(q_gain[g]), _col(k_gain[g]), tt=tt, name=f"dil_inproj_d{dil}"))
    head_of_col = np.arange(d_model) // HEAD_DIM
    expand = (np.arange(2 * HEAD_DIM)[:, None] == head_of_col[None, :]).astype(np.float32)
    expand = jnp.asarray(np.concatenate([expand, expand], axis=0), BF16)
    assert n_heads <= 2 * HEAD_DIM
    w_out = w_out.astype(BF16)

    def attend_and_merge(bounded):
        outs, stats = [], []
        for (q, kt, v), dil in zip(qkvs, dils):
            o, stat = _dil_attn(q, kt, v, n_seqs=batch * dil, tt=tt, bounded=bounded,
                                name=f"dil_attn_d{dil}" + ("" if bounded else "_exact"))
            outs.append(o)
            stats.append(stat)
        return _dil_out_mlp(outs, stats, expand, w_out, x2, *mlp, batch=batch, seq=seq, tt=tt, dils=dils,
                            bounded=bounded)

    bound = (math.sqrt(HEAD_DIM) * LOG2E * (1.0 + 2.0 ** -6)
             * jnp.max(jnp.max(jnp.abs(q_gain), axis=1) * jnp.max(jnp.abs(k_gain), axis=1)))
    return lax.cond(bound <= DIL_MAX_SCORE_LOG2, functools.partial(attend_and_merge, True),
                    functools.partial(attend_and_merge, False))


def _forward(x, positions, fox_norm, fox_w_in, fox_b_f, fox_q_gain, fox_k_gain, fox_w_out,
             dil_norm, dil_w_in, dil_q_gain, dil_k_gain, dil_w_out, mlp_norm, mlp_w1, mlp_w2,
             *, tt, tk_fox, unroll_fox):
    batch, seq, d_model = x.shape
    assert seq % (tt * max(dil for _, dil in DIL_GROUPS)) == 0 and tt % tk_fox == 0
    x2 = x.reshape(batch * seq, d_model)
    depth = mlp_norm.shape[0]
    for layer in range(depth):
        j = layer // 2
        mlp = (mlp_norm[layer].reshape(1, -1), mlp_w1[layer].astype(BF16), mlp_w2[layer].astype(BF16))
        if layer % 2 == 0:
            x2 = _fox_layer(x2, fox_norm[j], fox_w_in[j], fox_b_f[j], fox_q_gain[j], fox_k_gain[j],
                            fox_w_out[j], mlp, batch=batch, seq=seq, tt=tt, tk=tk_fox, unroll=unroll_fox)
        else:
            x2 = _dil_layer(x2, positions, dil_norm[j], dil_w_in[j], dil_q_gain[j], dil_k_gain[j],
                            dil_w_out[j], mlp, batch=batch, seq=seq, tt=tt)
    return x2.reshape(batch, seq, d_model)


def kernel(x, positions, fox_norm, fox_w_in, fox_b_f, fox_q_gain, fox_k_gain, fox_w_out, dil_norm, dil_w_in, dil_q_gain, dil_k_gain, dil_w_out, mlp_norm, mlp_w1, mlp_w2):
    return _forward(x, positions, fox_norm, fox_w_in, fox_b_f, fox_q_gain, fox_k_gain, fox_w_out,
                    dil_norm, dil_w_in, dil_q_gain, dil_k_gain, dil_w_out, mlp_norm, mlp_w1, mlp_w2,
                    tt=512, tk_fox=256, unroll_fox=(12, 6, 2))
```

```python
import functools
import math

import jax
import jax.numpy as jnp
import numpy as np
from jax import lax
from jax.experimental import pallas as pl
from jax.experimental.pallas import tpu as pltpu

F32, BF16 = jnp.float32, jnp.bfloat16
HEAD_DIM = 64
ROT_DIM = 16
ROPE_THETA = 500000.0
DIL_GROUPS = ((128, 1), (512, 4), (2048, 16))
DIL_BLOCK = 128
EPS = 1e-6
NEG_INF = -1e30
LOG2E = 1.4426950408889634
LANES = 128
BIAS_ROWS = 3
FOX_SKIP_LOG2 = 160.0
FOX_MAX_LOOSENESS_LOG2 = 100.0
DIL_MAX_SCORE_LOG2 = 50.0
VMEM_LIMIT = 56 * 1024 * 1024

_NT = (((1,), (1,)), ((), ()))
_TN = (((0,), (0,)), ((), ()))


def _params(sem):
    return pltpu.CompilerParams(dimension_semantics=sem, vmem_limit_bytes=VMEM_LIMIT)


def _rms(x, g):
    return x * lax.rsqrt(jnp.mean(x * x, axis=-1, keepdims=True) + EPS) * g


def _head_norm(y, gain):
    inv = lax.rsqrt(jnp.mean(y * y, axis=1, keepdims=True) + EPS)
    return y * inv * gain[None]


def _split3(x):
    hi = x.astype(BF16).astype(F32)
    rem = x - hi
    mid = rem.astype(BF16).astype(F32)
    return hi, mid, rem - mid


def _fox_inproj_kernel(x_ref, g_ref, wt_ref, wf_ref, bf_ref, gq_ref, gk_ref,
                       qkv_ref, c_ref, carry_ref, *, tiles_per_batch):
    i = pl.program_id(0)

    @pl.when(i % tiles_per_batch == 0)
    def _():
        carry_ref[...] = jnp.zeros_like(carry_ref)

    h = _rms(x_ref[...], g_ref[...]).astype(BF16)
    tt, d_model = x_ref.shape
    n_heads = d_model // HEAD_DIM

    f = lax.dot_general(wf_ref[...], h, _NT, preferred_element_type=F32) + bf_ref[...]
    lf = (jnp.minimum(f, 0.0) - jnp.log1p(jnp.exp(-jnp.abs(f)))) * LOG2E
    lane = lax.broadcasted_iota(jnp.int32, lf.shape, 1)
    shift = 1
    while shift < tt:
        lf = lf + jnp.where(lane >= shift, pltpu.roll(lf, shift, 1), 0.0)
        shift *= 2
    c = lf + carry_ref[:, 0:1]
    c_ref[...] = c
    carry_ref[...] = jnp.broadcast_to(c[:, tt - 1:tt], carry_ref.shape)

    for part, gain_ref, mult in ((0, gq_ref, LOG2E / math.sqrt(HEAD_DIM)), (1, gk_ref, 1.0)):
        rows = slice(part * d_model, (part + 1) * d_model)
        y = lax.dot_general(wt_ref[rows, :], h, _NT, preferred_element_type=F32)
        y = _head_norm(y.reshape(n_heads, HEAD_DIM, tt), gain_ref[...] * mult)
        qkv_ref[rows, :] = y.reshape(d_model, tt).astype(BF16)
    rows = slice(2 * d_model, 3 * d_model)
    qkv_ref[rows, :] = lax.dot_general(wt_ref[rows, :], h, _NT, preferred_element_type=F32).astype(BF16)


def _fox_inproj(x2, g, wt, wf, bf, gq, gk, *, seq, tt):
    tokens, d_model = x2.shape
    n_heads = d_model // HEAD_DIM
    nt = tokens // tt
    const = lambda i: (0, 0)
    return pl.pallas_call(
        functools.partial(_fox_inproj_kernel, tiles_per_batch=seq // tt),
        grid=(nt,),
        in_specs=[
            pl.BlockSpec((tt, d_model), lambda i: (i, 0)),
            pl.BlockSpec((1, d_model), const),
            pl.BlockSpec((3 * d_model, d_model), const),
            pl.BlockSpec((n_heads, d_model), const),
            pl.BlockSpec((n_heads, 1), const),
            pl.BlockSpec((HEAD_DIM, 1), const),
            pl.BlockSpec((HEAD_DIM, 1), const),
        ],
        out_specs=[
            pl.BlockSpec((None, 3 * d_model, tt), lambda i: (i, 0, 0)),
            pl.BlockSpec((None, n_heads, tt), lambda i: (i, 0, 0)),
        ],
        out_shape=[
            jax.ShapeDtypeStruct((nt, 3 * d_model, tt), BF16),
            jax.ShapeDtypeStruct((nt, n_heads, tt), F32),
        ],
        scratch_shapes=[pltpu.VMEM((n_heads, LANES), F32)],
        compiler_params=_params(("arbitrary",)),
        name="fox_inproj",
    )(x2, g, wt, wf, bf, gq, gk)


def _build_keys(k_ref, c_ref, kp_ref, extra_rows):
    tiles, _, tt = k_ref.shape
    pad = jnp.zeros((HEAD_DIM - BIAS_ROWS - extra_rows, tt), F32)
    ones = [jnp.ones((extra_rows, tt), F32)] if extra_rows else []
    for n in range(tiles):
        blk = jnp.concatenate([k_ref[n].astype(F32), *_split3(-c_ref[n]), *ones, pad], axis=0)
        kp_ref[n * tt:(n + 1) * tt, :] = blk.T.astype(BF16)


def _fox_exact_kernel(q_ref, k_ref, v_ref, c_ref, o_ref, kp_ref):
    i = pl.program_id(2)
    tt = q_ref.shape[1]

    @pl.when(i == 0)
    def _():
        _build_keys(k_ref, c_ref, kp_ref, 0)

    ones_rows = (lax.broadcasted_iota(jnp.int32, (HEAD_DIM, tt), 0) < BIAS_ROWS).astype(BF16)
    qp = jnp.concatenate([q_ref[...], ones_rows], axis=0)

    def step(j, carry, masked):
        m, l, acc = carry
        off = pl.multiple_of(j * tt, tt)
        s = jnp.dot(kp_ref[pl.ds(off, tt), :], qp, preferred_element_type=F32)
        if masked:
            key = lax.broadcasted_iota(jnp.int32, s.shape, 0)
            qry = lax.broadcasted_iota(jnp.int32, s.shape, 1)
            s = jnp.where(key <= qry, s, NEG_INF)
        m_new = jnp.maximum(m, jnp.max(s, axis=0, keepdims=True))
        alpha = jnp.exp2(m - m_new)
        p = jnp.exp2(s - m_new)
        l = alpha * l + jnp.sum(p, axis=0, keepdims=True)
        pv = jnp.dot(v_ref[j], p.astype(BF16), preferred_element_type=F32)
        return m_new, l, alpha * acc + pv

    init = (jnp.full((1, tt), NEG_INF, F32), jnp.zeros((1, tt), F32), jnp.zeros((HEAD_DIM, tt), F32))
    carry = lax.fori_loop(0, i, lambda j, c: step(j, c, False), init)
    m, l, acc = step(i, carry, True)
    o_ref[...] = (acc / l).astype(BF16)


def _fox_fast_kernel(jlo_ref, kb_ref, q_ref, k_ref, v_ref, c_ref, o_ref, kp_ref, qp_ref, *s_refs, tk, unroll):
    bh = pl.program_id(0) * pl.num_programs(1) + pl.program_id(1)
    nq, _, tq = q_ref.shape
    per, nkv = tq // tk, nq * (tq // tk)
    kb = kb_ref[0]
    pad_rows = HEAD_DIM - 2 * BIAS_ROWS

    _build_keys(k_ref, c_ref, kp_ref, BIAS_ROWS)
    lane = lax.broadcasted_iota(jnp.int32, (tk, 2 * HEAD_DIM), 1)
    kp_ref[nq * tq:nq * tq + tk, :] = jnp.where(lane == HEAD_DIM, NEG_INF, 0.0).astype(BF16)

    key = lax.broadcasted_iota(jnp.int32, (tk, tq), 0)
    qry = lax.broadcasted_iota(jnp.int32, (tk, tq), 1)

    def qk(blk, qp):
        off = pl.multiple_of(blk * tk, tk)
        return jnp.dot(kp_ref[pl.ds(off, tk), :], qp, preferred_element_type=F32)

    def prologue(i):
        qt = q_ref[i].astype(F32)
        qn = jnp.sqrt(jnp.sum(qt * qt, axis=0, keepdims=True))
        neg_m = c_ref[i] - qn * kb
        qp = jnp.concatenate([qt, jnp.ones((BIAS_ROWS, tq), F32), *_split3(neg_m),
                              jnp.zeros((pad_rows, tq), F32)], axis=0).astype(BF16)
        qp_ref[...] = qp
        return [jnp.where(key + t * tk <= qry, qk(i * per + t, qp), NEG_INF) for t in range(per)]

    for t, tile in enumerate(prologue(0)):
        s_refs[t][...] = tile

    def q_block(i, _):
        qp = qp_ref[...]
        jlo = jlo_ref[bh * nq + i]
        n_off = i * per - jlo

        def pv(blk, s, carry):
            acc, l = carry
            off = pl.multiple_of(lax.rem(blk, per) * tk, tk)
            p = jnp.exp2(s)
            l = l + jnp.sum(p.reshape(tk // 8, 8, tq), axis=0)
            v_tile = v_ref[lax.div(blk, per), :, pl.ds(off, tk)]
            return acc + jnp.dot(v_tile, p.astype(BF16), preferred_element_type=F32), l

        def off_tile(g):
            return jnp.where(g < n_off, jlo + g, nkv), jnp.where(g < n_off, jlo + g, 0)

        def consumed_v(g):
            return jnp.where(g < per, i * per + g, off_tile(g - per)[1])

        def body(width, base, u, carry):
            pend = [s_refs[t][...] for t in range(per)]
            for t in range(width):
                g = base + u * width + t
                pend.append(qk(off_tile(g)[0], qp))
                carry = pv(consumed_v(g), pend.pop(0), carry)
            for t in range(per):
                s_refs[t][...] = pend[t]
            return carry

        carry = (jnp.zeros((HEAD_DIM, tq), F32), jnp.zeros((8, tq), F32))
        done = 0
        for level, width in enumerate(unroll):
            pad = width - 1 if level == len(unroll) - 1 else 0
            trips = (n_off - done + pad) // width
            carry = lax.fori_loop(0, trips, functools.partial(body, width, done), carry)
            done = done + trips * width
        next_tiles = prologue(jnp.minimum(i + 1, nq - 1))
        for t in range(per):
            carry = pv(consumed_v(done + t), s_refs[t][...], carry)
        acc, l = carry
        o_ref[i] = (acc / jnp.sum(l, axis=0, keepdims=True)).astype(BF16)
        for t in range(per):
            s_refs[t][...] = next_tiles[t]
        return 0

    lax.fori_loop(0, nq, q_block, 0)


def _fox_attn_fast(qkv, c4, jlo, kb, *, batch, seq, tk, unroll):
    nt, rows, tt = qkv.shape
    d_model = rows // 3
    n_heads = d_model // HEAD_DIM
    tpb = seq // tt
    grid_spec = pltpu.PrefetchScalarGridSpec(
        num_scalar_prefetch=1,
        grid=(batch, n_heads),
        in_specs=[
            pl.BlockSpec(memory_space=pltpu.SMEM),
            pl.BlockSpec((tpb, HEAD_DIM, tt), lambda b, h, jlo: (b, h, 0)),
            pl.BlockSpec((tpb, HEAD_DIM, tt), lambda b, h, jlo: (b, n_heads + h, 0)),
            pl.BlockSpec((tpb, HEAD_DIM, tt), lambda b, h, jlo: (b, 2 * n_heads + h, 0)),
            pl.BlockSpec((tpb, None, 1, tt), lambda b, h, jlo: (b, h, 0, 0)),
        ],
        out_specs=pl.BlockSpec((tpb, HEAD_DIM, tt), lambda b, h, jlo: (b, h, 0)),
        scratch_shapes=[pltpu.VMEM((seq + tk, 2 * HEAD_DIM), BF16), pltpu.VMEM((2 * HEAD_DIM, tt), BF16)]
        + [pltpu.VMEM((tk, tt), F32)] * (tt // tk),
    )
    return pl.pallas_call(
        functools.partial(_fox_fast_kernel, tk=tk, unroll=unroll),
        grid_spec=grid_spec,
        out_shape=jax.ShapeDtypeStruct((nt, d_model, tt), BF16),
        compiler_params=_params(("arbitrary", "arbitrary")),
        name="fox_attn_fast",
    )(jlo, kb, qkv, qkv, qkv, c4)


def _fox_attn_exact(qkv, c4, *, batch, seq):
    nt, rows, tt = qkv.shape
    d_model = rows // 3
    n_heads = d_model // HEAD_DIM
    tpb = seq // tt
    return pl.pallas_call(
        _fox_exact_kernel,
        grid=(batch, n_heads, tpb),
        in_specs=[
            pl.BlockSpec((None, HEAD_DIM, tt), lambda b, h, i: (b * tpb + i, h, 0)),
            pl.BlockSpec((tpb, HEAD_DIM, tt), lambda b, h, i: (b, n_heads + h, 0)),
            pl.BlockSpec((tpb, HEAD_DIM, tt), lambda b, h, i: (b, 2 * n_heads + h, 0)),
            pl.BlockSpec((tpb, None, 1, tt), lambda b, h, i: (b, h, 0, 0)),
        ],
        out_specs=pl.BlockSpec((None, HEAD_DIM, tt), lambda b, h, i: (b * tpb + i, h, 0)),
        out_shape=jax.ShapeDtypeStruct((nt, d_model, tt), BF16),
        scratch_shapes=[pltpu.VMEM((seq, 2 * HEAD_DIM), BF16)],
        compiler_params=_params(("arbitrary", "arbitrary", "arbitrary")),
        name="fox_attn_exact",
    )(qkv, qkv, qkv, c4)


def _resident(shape):
    return pl.BlockSpec(shape, lambda i: (0,) * len(shape), pipeline_mode=pl.Buffered(1))


def _mlp_residual(x1, g_ref, w1_ref, w2_ref):
    h = _rms(x1, g_ref[...]).astype(BF16)
    a = jnp.dot(h, w1_ref[...], preferred_element_type=F32)
    a = jnp.square(jnp.maximum(a, 0.0)).astype(BF16)
    return x1 + jnp.dot(a, w2_ref[...], preferred_element_type=F32)


def _mlp_specs(d_model, d_ff):
    return [_resident((1, d_model)), _resident((d_model, d_ff)), _resident((d_ff, d_model))]


def _fox_out_mlp_kernel(ot_ref, w_ref, x_ref, g_ref, w1_ref, w2_ref, y_ref):
    x1 = x_ref[...] + lax.dot_general(ot_ref[...], w_ref[...], _TN, preferred_element_type=F32)
    y_ref[...] = _mlp_residual(x1, g_ref, w1_ref, w2_ref)


def _fox_out_mlp(ot, w, x2, g, w1, w2):
    nt, d_model, tt = ot.shape
    return pl.pallas_call(
        _fox_out_mlp_kernel,
        grid=(nt,),
        in_specs=[
            pl.BlockSpec((None, d_model, tt), lambda i: (i, 0, 0)),
            _resident((d_model, d_model)),
            pl.BlockSpec((tt, d_model), lambda i: (i, 0)),
        ] + _mlp_specs(d_model, w1.shape[1]),
        out_specs=pl.BlockSpec((tt, d_model), lambda i: (i, 0)),
        out_shape=jax.ShapeDtypeStruct(x2.shape, F32),
        compiler_params=_params(("arbitrary",)),
        name="fox_outproj_mlp",
    )(ot, w, x2, g, w1, w2)


def _dil_prep_kernel(x_ref, g_ref, *refs, dils):
    out_refs, slab_ref = refs[:-1], refs[-1]
    h = _rms(x_ref[...], g_ref[...])
    tt, d_model = h.shape
    n_slabs = d_model // LANES
    if any(d > 1 for d in dils):
        for j in range(n_slabs):
            slab_ref[j] = h[:, j * LANES:(j + 1) * LANES]
    for dil, o_ref in zip(dils, out_refs):
        if dil == 1:
            o_ref[...] = h.astype(BF16)
            continue
        n = tt // dil
        for r in range(dil):
            rows = [slab_ref[j, pl.ds(r, n, stride=dil), :] for j in range(n_slabs)]
            o_ref[r] = jnp.concatenate(rows, axis=1).astype(BF16)


def _dil_prep(x2, g, *, batch, seq, tt, dils):
    tokens, d_model = x2.shape
    tpb = seq // tt
    out_specs, out_shape = [], []
    for dil in dils:
        if dil == 1:
            out_specs.append(pl.BlockSpec((tt, d_model), lambda i: (i, 0)))
            out_shape.append(jax.ShapeDtypeStruct((tokens, d_model), BF16))
        else:
            out_specs.append(pl.BlockSpec((None, dil, tt // dil, d_model), lambda i: (i // tpb, 0, i % tpb, 0)))
            out_shape.append(jax.ShapeDtypeStruct((batch, dil, seq // dil, d_model), BF16))
    return pl.pallas_call(
        functools.partial(_dil_prep_kernel, dils=dils),
        grid=(tokens // tt,),
        in_specs=[pl.BlockSpec((tt, d_model), lambda i: (i, 0)), pl.BlockSpec((1, d_model), lambda i: (0, 0))],
        out_specs=out_specs,
        out_shape=out_shape,
        scratch_shapes=[pltpu.VMEM((d_model // LANES, tt, LANES), F32)],
        compiler_params=_params(("arbitrary",)),
        name="dil_prep",
    )(x2, g)


_TWO_PI = 2.0 * math.pi
_PI2_A = 6.28125
_PI2_B = float((np.array(_TWO_PI - _PI2_A, np.float32).view(np.uint32) & np.uint32(0xFFFFF000)).view(np.float32))
_PI2_C = _TWO_PI - _PI2_A - _PI2_B


def _rope_tables(pos_row, inv_col):
    ang = pos_row.astype(F32) * inv_col
    k = jnp.floor(ang * (1.0 / _TWO_PI) + 0.5)
    r = ((ang - k * _PI2_A) - k * _PI2_B) - k * _PI2_C
    return jnp.cos(r), jnp.sin(r)


def _dil_inproj_kernel(h_ref, pos_ref, inv_ref, wqk_ref, wv_ref, gq_ref, gk_ref, q_ref, kt_ref, v_ref):
    h = h_ref[...]
    tt, d_model = h.shape
    n_heads = d_model // HEAD_DIM
    half = ROT_DIM // 2
    cos, sin = _rope_tables(pos_ref[...], inv_ref[...])
    for part, gain_ref, mult in ((0, gq_ref, LOG2E / math.sqrt(HEAD_DIM)), (1, gk_ref, 1.0)):
        rows = slice(part * d_model, (part + 1) * d_model)
        y = lax.dot_general(wqk_ref[rows, :], h, _NT, preferred_element_type=F32)
        y = _head_norm(y.reshape(n_heads, HEAD_DIM, tt), gain_ref[...] * mult)
        x1, x2 = y[:, 0:half], y[:, half:ROT_DIM]
        y = jnp.concatenate([x1 * cos - x2 * sin, x2 * cos + x1 * sin, y[:, ROT_DIM:]], axis=1)
        y = y.reshape(d_model, tt)
        if part == 0:
            q_ref[...] = y.T.astype(BF16)
        else:
            kt_ref[...] = y.astype(BF16)
    v_ref[...] = jnp.dot(h, wv_ref[...], preferred_element_type=F32).astype(BF16)


def _dil_inproj(h2, pos_perm, inv, wqk, wv, gq, gk, *, tt, name):
    tokens, d_model = h2.shape
    nt = tokens // tt
    const = lambda i: (0, 0)
    return pl.pallas_call(
        _dil_inproj_kernel,
        grid=(nt,),
        in_specs=[
            pl.BlockSpec((tt, d_model), lambda i: (i, 0)),
            pl.BlockSpec((1, tt), lambda i: (0, i)),
            pl.BlockSpec((ROT_DIM // 2, 1), const),
            pl.BlockSpec((2 * d_model, d_model), const),
            pl.BlockSpec((d_model, d_model), const),
            pl.BlockSpec((HEAD_DIM, 1), const),
            pl.BlockSpec((HEAD_DIM, 1), const),
        ],
        out_specs=[
            pl.BlockSpec((tt, d_model), lambda i: (i, 0)),
            pl.BlockSpec((None, d_model, tt), lambda i: (i, 0, 0)),
            pl.BlockSpec((tt, d_model), lambda i: (i, 0)),
        ],
        out_shape=[
            jax.ShapeDtypeStruct((tokens, d_model), BF16),
            jax.ShapeDtypeStruct((nt, d_model, tt), BF16),
            jax.ShapeDtypeStruct((tokens, d_model), BF16),
        ],
        compiler_params=_params(("arbitrary",)),
        name=name,
    )(h2, pos_perm, inv, wqk, wv, gq, gk)


def _dil_attn_kernel(q_ref, ktp_ref, ktc_ref, vp_ref, vc_ref, o_ref, stat_ref, *, bounded):
    i = pl.program_id(1)
    blk = DIL_BLOCK
    has_prev = i > 0
    tq, d_model = q_ref.shape
    kt_all = jnp.concatenate([ktp_ref[...], ktc_ref[...]], axis=1)
    v_all = jnp.concatenate([vp_ref[...], vc_ref[...]], axis=0)
    r = lax.broadcasted_iota(jnp.int32, (blk, 2 * blk), 0)
    c = lax.broadcasted_iota(jnp.int32, (blk, 2 * blk), 1)
    band = (c >= r) & (c <= r + blk)
    first_mask = band & (has_prev | (c >= blk))
    lane = lax.broadcasted_iota(jnp.int32, (blk, 2 * HEAD_DIM), 1)
    low = lane < HEAD_DIM
    for u in range(tq // blk):
        rows = slice(u * blk, (u + 1) * blk)
        mask = first_mask if u == 0 else band
        stat_tile = jnp.zeros((blk, 2 * HEAD_DIM), F32)
        for pair in range(d_model // (2 * HEAD_DIM)):
            cols = slice(pair * 2 * HEAD_DIM, (pair + 1) * 2 * HEAD_DIM)
            qp = q_ref[rows, cols]
            kt = kt_all[cols, u * blk:(u + 2) * blk]
            vv = v_all[u * blk:(u + 2) * blk, cols]
            outs = []
            for hh, sel in ((0, low), (1, ~low)):
                qh = jnp.where(sel, qp, jnp.zeros_like(qp))
                s = jnp.dot(qh, kt, preferred_element_type=F32)
                s = jnp.where(mask, s, NEG_INF)
                if bounded:
                    p = jnp.exp2(s)
                    stat = jnp.sum(p, axis=1, keepdims=True)
                    outs.append(jnp.dot(p.astype(BF16), vv, preferred_element_type=F32))
                else:
                    m = jnp.max(s, axis=1, keepdims=True)
                    p = jnp.exp2(s - m)
                    den = jnp.sum(p, axis=1, keepdims=True)
                    stat = m + jnp.log2(den)
                    outs.append(jnp.dot(p.astype(BF16), vv, preferred_element_type=F32) * (1.0 / den))
                stat_tile = jnp.where(lane == 2 * pair + hh, stat, stat_tile)
            o_ref[rows, cols] = jnp.where(low, outs[0], outs[1]).astype(BF16)
        stat_ref[rows, :] = stat_tile


def _dil_attn(q, kt, v, *, n_seqs, tt, bounded, name):
    tokens, d_model = q.shape
    nq = tokens // tt // n_seqs
    per = tt // DIL_BLOCK
    tile = lambda s, i: s * nq + i
    prev_tile = lambda s, i: jnp.maximum(tile(s, i) - 1, 0)
    return pl.pallas_call(
        functools.partial(_dil_attn_kernel, bounded=bounded),
        grid=(n_seqs, nq),
        in_specs=[
            pl.BlockSpec((tt, d_model), lambda s, i: (tile(s, i), 0)),
            pl.BlockSpec((None, d_model, DIL_BLOCK), lambda s, i: (prev_tile(s, i), 0, per - 1)),
            pl.BlockSpec((None, d_model, tt), lambda s, i: (tile(s, i), 0, 0)),
            pl.BlockSpec((DIL_BLOCK, d_model), lambda s, i: ((prev_tile(s, i) + 1) * per - 1, 0)),
            pl.BlockSpec((tt, d_model), lambda s, i: (tile(s, i), 0)),
        ],
        out_specs=[
            pl.BlockSpec((tt, d_model), lambda s, i: (tile(s, i), 0)),
            pl.BlockSpec((tt, 2 * HEAD_DIM), lambda s, i: (tile(s, i), 0)),
        ],
        out_shape=[
            jax.ShapeDtypeStruct((tokens, d_model), BF16),
            jax.ShapeDtypeStruct((tokens, 2 * HEAD_DIM), F32),
        ],
        compiler_params=_params(("arbitrary", "arbitrary")),
        name=name,
    )(q, kt, kt, v, v)


def _dil_out_kernel(*refs, dils, bounded):
    n = len(dils)
    o_refs, stat_refs = refs[:n], refs[n:2 * n]
    e_ref, w_ref, x_ref, g_ref, w1_ref, w2_ref, y_ref, o_slab, stat_slab = refs[2 * n:]
    tt, d_model = x_ref.shape

    def natural(ref, dil, slab):
        if dil == 1:
            return ref[...].astype(F32)
        rows, width = tt // dil, ref.shape[-1]
        for r in range(dil):
            val = ref[r].astype(F32)
            for j in range(width // LANES):
                slab[j, pl.ds(r, rows, stride=dil), :] = val[:, j * LANES:(j + 1) * LANES]
        return jnp.concatenate([slab[j] for j in range(width // LANES)], axis=1)

    stats = [natural(ref, dil, stat_slab) for ref, dil in zip(stat_refs, dils)]
    outs = lambda g: natural(o_refs[g], dils[g], o_slab)

    def per_head_to_cols(w):
        hi = w.astype(BF16)
        lo = (w - hi.astype(F32)).astype(BF16)
        return jnp.dot(jnp.concatenate([hi, lo], axis=1), e_ref[...], preferred_element_type=F32)

    if bounded:
        lane = lax.broadcasted_iota(jnp.int32, stats[0].shape, 1)
        den = jnp.where(lane < d_model // HEAD_DIM, sum(stats[1:], stats[0]), 1.0)
        num = outs(0)
        for g in range(1, n):
            num = num + outs(g)
        merged = num * per_head_to_cols(1.0 / den)
    else:
        top = functools.reduce(jnp.maximum, stats)
        es = [jnp.exp2(l - top) for l in stats]
        den = sum(es[1:], es[0])
        merged = None
        for g in range(n):
            term = per_head_to_cols(es[g] / den) * outs(g)
            merged = term if merged is None else merged + term
    x1 = x_ref[...] + jnp.dot(merged.astype(BF16), w_ref[...], preferred_element_type=F32)
    y_ref[...] = _mlp_residual(x1, g_ref, w1_ref, w2_ref)


def _dil_out_mlp(outs, stats, expand, w, x2, g, w1, w2, *, batch, seq, tt, dils, bounded):
    tokens, d_model = x2.shape
    tpb = seq // tt

    def view_and_spec(a, dil):
        width = a.shape[-1]
        if dil == 1:
            return a, pl.BlockSpec((tt, width), lambda i: (i, 0))
        spec = pl.BlockSpec((None, dil, tt // dil, width), lambda i: (i // tpb, 0, i % tpb, 0))
        return a.reshape(batch, dil, seq // dil, width), spec

    args, specs = [], []
    for group in (outs, stats):
        for a, dil in zip(group, dils):
            a, spec = view_and_spec(a, dil)
            args.append(a)
            specs.append(spec)
    row = lambda i: (i, 0)
    return pl.pallas_call(
        functools.partial(_dil_out_kernel, dils=dils, bounded=bounded),
        grid=(tokens // tt,),
        in_specs=specs + [
            _resident((4 * HEAD_DIM, d_model)),
            _resident((d_model, d_model)),
            pl.BlockSpec((tt, d_model), row),
        ] + _mlp_specs(d_model, w1.shape[1]),
        out_specs=pl.BlockSpec((tt, d_model), row),
        out_shape=jax.ShapeDtypeStruct((tokens, d_model), F32),
        scratch_shapes=[pltpu.VMEM((d_model // LANES, tt, LANES), F32), pltpu.VMEM((1, tt, LANES), F32)],
        compiler_params=_params(("arbitrary",)),
        name="dil_outproj_mlp" + ("" if bounded else "_exact"),
    )(*args, expand, w, x2, g, w1, w2)


def _col(v):
    return v.reshape(-1, 1).astype(F32)


def _fox_layer(x2, norm_g, w_in, b_f, q_gain, k_gain, w_out, mlp, *, batch, seq, tt, tk, unroll):
    d_model = x2.shape[1]
    n_heads = d_model // HEAD_DIM
    tpb = seq // tt
    wt = w_in[:, :3 * d_model].T.astype(BF16)
    wf = w_in[:, 3 * d_model:].T.astype(BF16)
    qkv, c = _fox_inproj(x2, norm_g.reshape(1, -1), wt, wf, _col(b_f), _col(q_gain), _col(k_gain),
                         seq=seq, tt=tt)
    c4 = c.reshape(batch * tpb, n_heads, 1, tt)

    slack = 1.0 + 2.0 ** -7
    qb = math.sqrt(HEAD_DIM) * jnp.max(jnp.abs(q_gain)) * (LOG2E / math.sqrt(HEAD_DIM)) * slack
    kb = math.sqrt(HEAD_DIM) * jnp.max(jnp.abs(k_gain)) * slack
    per = tt // tk
    ch = c.reshape(batch, tpb, n_heads, tt).transpose(0, 2, 1, 3).reshape(batch, n_heads, seq)
    gap = ch[:, :, ::tt, None] - ch[:, :, None, tk - 1::tk]
    below = jnp.arange(seq // tk)[None, :] < (jnp.arange(tpb) * per)[:, None]
    jlo = jnp.sum((gap < -FOX_SKIP_LOG2) & below, axis=-1, dtype=jnp.int32).reshape(-1)

    ot = lax.cond(
        2.0 * qb * kb <= FOX_MAX_LOOSENESS_LOG2,
        lambda: _fox_attn_fast(qkv, c4, jlo, kb.reshape(1).astype(F32), batch=batch, seq=seq,
                               tk=tk, unroll=unroll),
        lambda: _fox_attn_exact(qkv, c4, batch=batch, seq=seq),
    )
    return _fox_out_mlp(ot, w_out.astype(BF16), x2, *mlp)


def _dil_layer(x2, positions, norm_g, w_in, q_gain, k_gain, w_out, mlp, *, batch, seq, tt):
    d_model = x2.shape[1]
    n_heads = d_model // HEAD_DIM
    dils = tuple(dil for _, dil in DIL_GROUPS)
    assert all(window // dil == DIL_BLOCK for window, dil in DIL_GROUPS)
    inv = (ROPE_THETA ** (-np.arange(ROT_DIM // 2, dtype=np.float64) * 2.0 / ROT_DIM)).astype(np.float32)
    inv = jnp.asarray(inv).reshape(-1, 1)
    hs = _dil_prep(x2, norm_g.reshape(1, -1), batch=batch, seq=seq, tt=tt, dils=dils)
    qkvs = []
    for g, dil in enumerate(dils):
        base = g * 3 * d_model
        wqk = w_in[:, base:base + 2 * d_model].T.astype(BF16)
        wv = w_in[:, base + 2 * d_model:base + 3 * d_model].astype(BF16)
        pos_perm = positions.reshape(batch, seq // dil, dil).transpose(0, 2, 1).reshape(1, -1)
        qkvs.append(_dil_inproj(hs[g].reshape(batch * seq, d_model), pos_perm, inv, wqk, wv,
                                _col(q_gain[g]), _col(k_gain[g]), tt=tt, name=f"dil_inproj_d{dil}"))
    head_of_col = np.arange(d_model) // HEAD_DIM
    expand = (np.arange(2 * HEAD_DIM)[:, None] == head_of_col[None, :]).astype(np.float32)
    expand = jnp.asarray(np.concatenate([expand, expand], axis=0), BF16)
    assert n_heads <= 2 * HEAD_DIM
    w_out = w_out.astype(BF16)

    def attend_and_merge(bounded):
        outs, stats = [], []
        for (q, kt, v), dil in zip(qkvs, dils):
            o, stat = _dil_attn(q, kt, v, n_seqs=batch * dil, tt=tt, bounded=bounded,
                                name=f"dil_attn_d{dil}" + ("" if bounded else "_exact"))
            outs.append(o)
            stats.append(stat)
        return _dil_out_mlp(outs, stats, expand, w_out, x2, *mlp, batch=batch, seq=seq, tt=tt, dils=dils,
                            bounded=bounded)

    bound = (math.sqrt(HEAD_DIM) * LOG2E * (1.0 + 2.0 ** -6)
             * jnp.max(jnp.max(jnp.abs(q_gain), axis=1) * jnp.max(jnp.abs(k_gain), axis=1)))
    return lax.cond(bound <= DIL_MAX_SCORE_LOG2, functools.partial(attend_and_merge, True),
                    functools.partial(attend_and_merge, False))


def _forward(x, positions, fox_norm, fox_w_in, fox_b_f, fox_q_gain, fox_k_gain, fox_w_out,
             dil_norm, dil_w_in, dil_q_gain, dil_k_gain, dil_w_out, mlp_norm, mlp_w1, mlp_w2,
             *, tt, tk_fox, unroll_fox):
    batch, seq, d_model = x.shape
    assert seq % (tt * max(dil for _, dil in DIL_GROUPS)) == 0 and tt % tk_fox == 0
    x2 = x.reshape(batch * seq, d_model)
    depth = mlp_norm.shape[0]
    for layer in range(depth):
        j = layer // 2
        mlp = (mlp_norm[layer].reshape(1, -1), mlp_w1[layer].astype(BF16), mlp_w2[layer].astype(BF16))
        if layer % 2 == 0:
            x2 = _fox_layer(x2, fox_norm[j], fox_w_in[j], fox_b_f[j], fox_q_gain[j], fox_k_gain[j],
                            fox_w_out[j], mlp, batch=batch, seq=seq, tt=tt, tk=tk_fox, unroll=unroll_fox)
        else:
            x2 = _dil_layer(x2, positions, dil_norm[j], dil_w_in[j], dil_q_gain[j], dil_k_gain[j],
                            dil_w_out[j], mlp, batch=batch, seq=seq, tt=tt)
    return x2.reshape(batch, seq, d_model)


def kernel(x, positions, fox_norm, fox_w_in, fox_b_f, fox_q_gain, fox_k_gain, fox_w_out, dil_norm, dil_w_in, dil_q_gain, dil_k_gain, dil_w_out, mlp_norm, mlp_w1, mlp_w2):
    return _forward(x, positions, fox_norm, fox_w_in, fox_b_f, fox_q_gain, fox_k_gain, fox_w_out,
                    dil_norm, dil_w_in, dil_q_gain, dil_k_gain, dil_w_out, mlp_norm, mlp_w1, mlp_w2,
                    tt=512, tk_fox=256, unroll_fox=(12, 6, 2))
```

```python
import functools
import math

import jax
import jax.numpy as jnp
import numpy as np
from jax import lax
from jax.experimental import pallas as pl
from jax.experimental.pallas import tpu as pltpu

F32, BF16 = jnp.float32, jnp.bfloat16
HEAD_DIM = 64
ROT_DIM = 16
ROPE_THETA = 500000.0
DIL_GROUPS = ((128, 1), (512, 4), (2048, 16))
DIL_BLOCK = 128
EPS = 1e-6
NEG_INF = -1e30
LOG2E = 1.4426950408889634
LANES = 128
BIAS_ROWS = 3
FOX_SKIP_LOG2 = 160.0
FOX_MAX_LOOSENESS_LOG2 = 100.0
DIL_MAX_SCORE_LOG2 = 50.0
VMEM_LIMIT = 56 * 1024 * 1024

_NT = (((1,), (1,)), ((), ()))
_TN = (((0,), (0,)), ((), ()))


def _params(sem):
    return pltpu.CompilerParams(dimension_semantics=sem, vmem_limit_bytes=VMEM_LIMIT)


def _rms(x, g):
    return x * lax.rsqrt(jnp.mean(x * x, axis=-1, keepdims=True) + EPS) * g


def _head_norm(y, gain):
    inv = lax.rsqrt(jnp.mean(y * y, axis=1, keepdims=True) + EPS)
    return y * inv * gain[None]


def _split3(x):
    hi = x.astype(BF16).astype(F32)
    rem = x - hi
    mid = rem.astype(BF16).astype(F32)
    return hi, mid, rem - mid


def _fox_inproj_kernel(x_ref, g_ref, wt_ref, wf_ref, bf_ref, gq_ref, gk_ref,
                       qkv_ref, c_ref, carry_ref, *, tiles_per_batch):
    i = pl.program_id(0)

    @pl.when(i % tiles_per_batch == 0)
    def _():
        carry_ref[...] = jnp.zeros_like(carry_ref)

    h = _rms(x_ref[...], g_ref[...]).astype(BF16)
    tt, d_model = x_ref.shape
    n_heads = d_model // HEAD_DIM

    f = lax.dot_general(wf_ref[...], h, _NT, preferred_element_type=F32) + bf_ref[...]
    lf = (jnp.minimum(f, 0.0) - jnp.log1p(jnp.exp(-jnp.abs(f)))) * LOG2E
    lane = lax.broadcasted_iota(jnp.int32, lf.shape, 1)
    shift = 1
    while shift < tt:
        lf = lf + jnp.where(lane >= shift, pltpu.roll(lf, shift, 1), 0.0)
        shift *= 2
    c = lf + carry_ref[:, 0:1]
    c_ref[...] = c
    carry_ref[...] = jnp.broadcast_to(c[:, tt - 1:tt], carry_ref.shape)

    for part, gain_ref, mult in ((0, gq_ref, LOG2E / math.sqrt(HEAD_DIM)), (1, gk_ref, 1.0)):
        rows = slice(part * d_model, (part + 1) * d_model)
        y = lax.dot_general(wt_ref[rows, :], h, _NT, preferred_element_type=F32)
        y = _head_norm(y.reshape(n_heads, HEAD_DIM, tt), gain_ref[...] * mult)
        qkv_ref[rows, :] = y.reshape(d_model, tt).astype(BF16)
    rows = slice(2 * d_model, 3 * d_model)
    qkv_ref[rows, :] = lax.dot_general(wt_ref[rows, :], h, _NT, preferred_element_type=F32).astype(BF16)


def _fox_inproj(x2, g, wt, wf, bf, gq, gk, *, seq, tt):
    tokens, d_model = x2.shape
    n_heads = d_model // HEAD_DIM
    nt = tokens // tt
    const = lambda i: (0, 0)
    return pl.pallas_call(
        functools.partial(_fox_inproj_kernel, tiles_per_batch=seq // tt),
        grid=(nt,),
        in_specs=[
            pl.BlockSpec((tt, d_model), lambda i: (i, 0)),
            pl.BlockSpec((1, d_model), const),
            pl.BlockSpec((3 * d_model, d_model), const),
            pl.BlockSpec((n_heads, d_model), const),
            pl.BlockSpec((n_heads, 1), const),
            pl.BlockSpec((HEAD_DIM, 1), const),
            pl.BlockSpec((HEAD_DIM, 1), const),
        ],
        out_specs=[
            pl.BlockSpec((None, 3 * d_model, tt), lambda i: (i, 0, 0)),
            pl.BlockSpec((None, n_heads, tt), lambda i: (i, 0, 0)),
        ],
        out_shape=[
            jax.ShapeDtypeStruct((nt, 3 * d_model, tt), BF16),
            jax.ShapeDtypeStruct((nt, n_heads, tt), F32),
        ],
        scratch_shapes=[pltpu.VMEM((n_heads, LANES), F32)],
        compiler_params=_params(("arbitrary",)),
        name="fox_inproj",
    )(x2, g, wt, wf, bf, gq, gk)


def _build_keys(k_ref, c_ref, kp_ref, extra_rows):
    tiles, _, tt = k_ref.shape
    pad = jnp.zeros((HEAD_DIM - BIAS_ROWS - extra_rows, tt), F32)
    ones = [jnp.ones((extra_rows, tt), F32)] if extra_rows else []
    for n in range(tiles):
        blk = jnp.concatenate([k_ref[n].astype(F32), *_split3(-c_ref[n]), *ones, pad], axis=0)
        kp_ref[n * tt:(n + 1) * tt, :] = blk.T.astype(BF16)


def _fox_exact_kernel(q_ref, k_ref, v_ref, c_ref, o_ref, kp_ref):
    i = pl.program_id(2)
    tt = q_ref.shape[1]

    @pl.when(i == 0)
    def _():
        _build_keys(k_ref, c_ref, kp_ref, 0)

    ones_rows = (lax.broadcasted_iota(jnp.int32, (HEAD_DIM, tt), 0) < BIAS_ROWS).astype(BF16)
    qp = jnp.concatenate([q_ref[...], ones_rows], axis=0)

    def step(j, carry, masked):
        m, l, acc = carry
        off = pl.multiple_of(j * tt, tt)
        s = jnp.dot(kp_ref[pl.ds(off, tt), :], qp, preferred_element_type=F32)
        if masked:
            key = lax.broadcasted_iota(jnp.int32, s.shape, 0)
            qry = lax.broadcasted_iota(jnp.int32, s.shape, 1)
            s = jnp.where(key <= qry, s, NEG_INF)
        m_new = jnp.maximum(m, jnp.max(s, axis=0, keepdims=True))
        alpha = jnp.exp2(m - m_new)
        p = jnp.exp2(s - m_new)
        l = alpha * l + jnp.sum(p, axis=0, keepdims=True)
        pv = jnp.dot(v_ref[j], p.astype(BF16), preferred_element_type=F32)
        return m_new, l, alpha * acc + pv

    init = (jnp.full((1, tt), NEG_INF, F32), jnp.zeros((1, tt), F32), jnp.zeros((HEAD_DIM, tt), F32))
    carry = lax.fori_loop(0, i, lambda j, c: step(j, c, False), init)
    m, l, acc = step(i, carry, True)
    o_ref[...] = (acc / l).astype(BF16)


def _fox_fast_kernel(jlo_ref, kb_ref, q_ref, k_ref, v_ref, c_ref, o_ref, kp_ref, qp_ref, *s_refs, tk, unroll):
    bh = pl.program_id(0) * pl.num_programs(1) + pl.program_id(1)
    nq, _, tq = q_ref.shape
    per, nkv = tq // tk, nq * (tq // tk)
    kb = kb_ref[0]
    pad_rows = HEAD_DIM - 2 * BIAS_ROWS

    _build_keys(k_ref, c_ref, kp_ref, BIAS_ROWS)
    lane = lax.broadcasted_iota(jnp.int32, (tk, 2 * HEAD_DIM), 1)
    kp_ref[nq * tq:nq * tq + tk, :] = jnp.where(lane == HEAD_DIM, NEG_INF, 0.0).astype(BF16)

    key = lax.broadcasted_iota(jnp.int32, (tk, tq), 0)
    qry = lax.broadcasted_iota(jnp.int32, (tk, tq), 1)

    def qk(blk, qp):
        off = pl.multiple_of(blk * tk, tk)
        return jnp.dot(kp_ref[pl.ds(off, tk), :], qp, preferred_element_type=F32)

    def prologue(i):
        qt = q_ref[i].astype(F32)
        qn = jnp.sqrt(jnp.sum(qt * qt, axis=0, keepdims=True))
        neg_m = c_ref[i] - qn * kb
        qp = jnp.concatenate([qt, jnp.ones((BIAS_ROWS, tq), F32), *_split3(neg_m),
                              jnp.zeros((pad_rows, tq), F32)], axis=0).astype(BF16)
        qp_ref[...] = qp
        return [jnp.where(key + t * tk <= qry, qk(i * per + t, qp), NEG_INF) for t in range(per)]

    for t, tile in enumerate(prologue(0)):
        s_refs[t][...] = tile

    def q_block(i, _):
        qp = qp_ref[...]
        jlo = jlo_ref[bh * nq + i]
        n_off = i * per - jlo

        def pv(blk, s, carry):
            acc, l = carry
            off = pl.multiple_of(lax.rem(blk, per) * tk, tk)
            p = jnp.exp2(s)
            l = l + jnp.sum(p.reshape(tk // 8, 8, tq), axis=0)
            v_tile = v_ref[lax.div(blk, per), :, pl.ds(off, tk)]
            return acc + jnp.dot(v_tile, p.astype(BF16), preferred_element_type=F32), l

        def off_tile(g):
            return jnp.where(g < n_off, jlo + g, nkv), jnp.where(g < n_off, jlo + g, 0)

        def consumed_v(g):
            return jnp.where(g < per, i * per + g, off_tile(g - per)[1])

        def body(width, base, u, carry):
            pend = [s_refs[t][...] for t in range(per)]
            for t in range(width):
                g = base + u * width + t
                pend.append(qk(off_tile(g)[0], qp))
                carry = pv(consumed_v(g), pend.pop(0), carry)
            for t in range(per):
                s_refs[t][...] = pend[t]
            return carry

        carry = (jnp.zeros((HEAD_DIM, tq), F32), jnp.zeros((8, tq), F32))
        done = 0
        for level, width in enumerate(unroll):
            pad = width - 1 if level == len(unroll) - 1 else 0
            trips = (n_off - done + pad) // width
            carry = lax.fori_loop(0, trips, functools.partial(body, width, done), carry)
            done = done + trips * width
        next_tiles = prologue(jnp.minimum(i + 1, nq - 1))
        for t in range(per):
            carry = pv(consumed_v(done + t), s_refs[t][...], carry)
        acc, l = carry
        o_ref[i] = (acc / jnp.sum(l, axis=0, keepdims=True)).astype(BF16)
        for t in range(per):
            s_refs[t][...] = next_tiles[t]
        return 0

    lax.fori_loop(0, nq, q_block, 0)


def _fox_attn_fast(qkv, c4, jlo, kb, *, batch, seq, tk, unroll):
    nt, rows, tt = qkv.shape
    d_model = rows // 3
    n_heads = d_model // HEAD_DIM
    tpb = seq // tt
    grid_spec = pltpu.PrefetchScalarGridSpec(
        num_scalar_prefetch=1,
        grid=(batch, n_heads),
        in_specs=[
            pl.BlockSpec(memory_space=pltpu.SMEM),
            pl.BlockSpec((tpb, HEAD_DIM, tt), lambda b, h, jlo: (b, h, 0)),
            pl.BlockSpec((tpb, HEAD_DIM, tt), lambda b, h, jlo: (b, n_heads + h, 0)),
            pl.BlockSpec((tpb, HEAD_DIM, tt), lambda b, h, jlo: (b, 2 * n_heads + h, 0)),
            pl.BlockSpec((tpb, None, 1, tt), lambda b, h, jlo: (b, h, 0, 0)),
        ],
        out_specs=pl.BlockSpec((tpb, HEAD_DIM, tt), lambda b, h, jlo: (b, h, 0)),
        scratch_shapes=[pltpu.VMEM((seq + tk, 2 * HEAD_DIM), BF16), pltpu.VMEM((2 * HEAD_DIM, tt), BF16)]
        + [pltpu.VMEM((tk, tt), F32)] * (tt // tk),
    )
    return pl.pallas_call(
        functools.partial(_fox_fast_kernel, tk=tk, unroll=unroll),
        grid_spec=grid_spec,
        out_shape=jax.ShapeDtypeStruct((nt, d_model, tt), BF16),
        compiler_params=_params(("arbitrary", "arbitrary")),
        name="fox_attn_fast",
    )(jlo, kb, qkv, qkv, qkv, c4)


def _fox_attn_exact(qkv, c4, *, batch, seq):
    nt, rows, tt = qkv.shape
    d_model = rows // 3
    n_heads = d_model // HEAD_DIM
    tpb = seq // tt
    return pl.pallas_call(
        _fox_exact_kernel,
        grid=(batch, n_heads, tpb),
        in_specs=[
            pl.BlockSpec((None, HEAD_DIM, tt), lambda b, h, i: (b * tpb + i, h, 0)),
            pl.BlockSpec((tpb, HEAD_DIM, tt), lambda b, h, i: (b, n_heads + h, 0)),
            pl.BlockSpec((tpb, HEAD_DIM, tt), lambda b, h, i: (b, 2 * n_heads + h, 0)),
            pl.BlockSpec((tpb, None, 1, tt), lambda b, h, i: (b, h, 0, 0)),
        ],
        out_specs=pl.BlockSpec((None, HEAD_DIM, tt), lambda b, h, i: (b * tpb + i, h, 0)),
        out_shape=jax.ShapeDtypeStruct((nt, d_model, tt), BF16),
        scratch_shapes=[pltpu.VMEM((seq, 2 * HEAD_DIM), BF16)],
        compiler_params=_params(("arbitrary", "arbitrary", "arbitrary")),
        name="fox_attn_exact",
    )(qkv, qkv, qkv, c4)


def _resident(shape):
    return pl.BlockSpec(shape, lambda i: (0,) * len(shape), pipeline_mode=pl.Buffered(1))


def _mlp_residual(x1, g_ref, w1_ref, w2_ref):
    h = _rms(x1, g_ref[...]).astype(BF16)
    a = jnp.dot(h, w1_ref[...], preferred_element_type=F32)
    a = jnp.square(jnp.maximum(a, 0.0)).astype(BF16)
    return x1 + jnp.dot(a, w2_ref[...], preferred_element_type=F32)


def _mlp_specs(d_model, d_ff):
    return [_resident((1, d_model)), _resident((d_model, d_ff)), _resident((d_ff, d_model))]


def _fox_out_mlp_kernel(ot_ref, w_ref, x_ref, g_ref, w1_ref, w2_ref, y_ref):
    x1 = x_ref[...] + lax.dot_general(ot_ref[...], w_ref[...], _TN, preferred_element_type=F32)
    y_ref[...] = _mlp_residual(x1, g_ref, w1_ref, w2_ref)


def _fox_out_mlp(ot, w, x2, g, w1, w2):
    nt, d_model, tt = ot.shape
    return pl.pallas_call(
        _fox_out_mlp_kernel,
        grid=(nt,),
        in_specs=[
            pl.BlockSpec((None, d_model, tt), lambda i: (i, 0, 0)),
            _resident((d_model, d_model)),
            pl.BlockSpec((tt, d_model), lambda i: (i, 0)),
        ] + _mlp_specs(d_model, w1.shape[1]),
        out_specs=pl.BlockSpec((tt, d_model), lambda i: (i, 0)),
        out_shape=jax.ShapeDtypeStruct(x2.shape, F32),
        compiler_params=_params(("arbitrary",)),
        name="fox_outproj_mlp",
    )(ot, w, x2, g, w1, w2)


def _dil_prep_kernel(x_ref, g_ref, *refs, dils):
    out_refs, slab_ref = refs[:-1], refs[-1]
    h = _rms(x_ref[...], g_ref[...])
    tt, d_model = h.shape
    n_slabs = d_model // LANES
    if any(d > 1 for d in dils):
        for j in range(n_slabs):
            slab_ref[j] = h[:, j * LANES:(j + 1) * LANES]
    for dil, o_ref in zip(dils, out_refs):
        if dil == 1:
            o_ref[...] = h.astype(BF16)
            continue
        n = tt // dil
        for r in range(dil):
            rows = [slab_ref[j, pl.ds(r, n, stride=dil), :] for j in range(n_slabs)]
            o_ref[r] = jnp.concatenate(rows, axis=1).astype(BF16)


def _dil_prep(x2, g, *, batch, seq, tt, dils):
    tokens, d_model = x2.shape
    tpb = seq // tt
    out_specs, out_shape = [], []
    for dil in dils:
        if dil == 1:
            out_specs.append(pl.BlockSpec((tt, d_model), lambda i: (i, 0)))
            out_shape.append(jax.ShapeDtypeStruct((tokens, d_model), BF16))
        else:
            out_specs.append(pl.BlockSpec((None, dil, tt // dil, d_model), lambda i: (i // tpb, 0, i % tpb, 0)))
            out_shape.append(jax.ShapeDtypeStruct((batch, dil, seq // dil, d_model), BF16))
    return pl.pallas_call(
        functools.partial(_dil_prep_kernel, dils=dils),
        grid=(tokens // tt,),
        in_specs=[pl.BlockSpec((tt, d_model), lambda i: (i, 0)), pl.BlockSpec((1, d_model), lambda i: (0, 0))],
        out_specs=out_specs,
        out_shape=out_shape,
        scratch_shapes=[pltpu.VMEM((d_model // LANES, tt, LANES), F32)],
        compiler_params=_params(("arbitrary",)),
        name="dil_prep",
    )(x2, g)


_TWO_PI = 2.0 * math.pi
_PI2_A = 6.28125
_PI2_B = float((np.array(_TWO_PI - _PI2_A, np.float32).view(np.uint32) & np.uint32(0xFFFFF000)).view(np.float32))
_PI2_C = _TWO_PI - _PI2_A - _PI2_B


def _rope_tables(pos_row, inv_col):
    ang = pos_row.astype(F32) * inv_col
    k = jnp.floor(ang * (1.0 / _TWO_PI) + 0.5)
    r = ((ang - k * _PI2_A) - k * _PI2_B) - k * _PI2_C
    return jnp.cos(r), jnp.sin(r)


def _dil_inproj_kernel(h_ref, pos_ref, inv_ref, wqk_ref, wv_ref, gq_ref, gk_ref, q_ref, kt_ref, v_ref):
    h = h_ref[...]
    tt, d_model = h.shape
    n_heads = d_model // HEAD_DIM
    half = ROT_DIM // 2
    cos, sin = _rope_tables(pos_ref[...], inv_ref[...])
    for part, gain_ref, mult in ((0, gq_ref, LOG2E / math.sqrt(HEAD_DIM)), (1, gk_ref, 1.0)):
        rows = slice(part * d_model, (part + 1) * d_model)
        y = lax.dot_general(wqk_ref[rows, :], h, _NT, preferred_element_type=F32)
        y = _head_norm(y.reshape(n_heads, HEAD_DIM, tt), gain_ref[...] * mult)
        x1, x2 = y[:, 0:half], y[:, half:ROT_DIM]
        y = jnp.concatenate([x1 * cos - x2 * sin, x2 * cos + x1 * sin, y[:, ROT_DIM:]], axis=1)
        y = y.reshape(d_model, tt)
        if part == 0:
            q_ref[...] = y.T.astype(BF16)
        else:
            kt_ref[...] = y.astype(BF16)
    v_ref[...] = jnp.dot(h, wv_ref[...], preferred_element_type=F32).astype(BF16)


def _dil_inproj(h2, pos_perm, inv, wqk, wv, gq, gk, *, tt, name):
    tokens, d_model = h2.shape
    nt = tokens // tt
    const = lambda i: (0, 0)
    return pl.pallas_call(
        _dil_inproj_kernel,
        grid=(nt,),
        in_specs=[
            pl.BlockSpec((tt, d_model), lambda i: (i, 0)),
            pl.BlockSpec((1, tt), lambda i: (0, i)),
            pl.BlockSpec((ROT_DIM // 2, 1), const),
            pl.BlockSpec((2 * d_model, d_model), const),
            pl.BlockSpec((d_model, d_model), const),
            pl.BlockSpec((HEAD_DIM, 1), const),
            pl.BlockSpec((HEAD_DIM, 1), const),
        ],
        out_specs=[
            pl.BlockSpec((tt, d_model), lambda i: (i, 0)),
            pl.BlockSpec((None, d_model, tt), lambda i: (i, 0, 0)),
            pl.BlockSpec((tt, d_model), lambda i: (i, 0)),
        ],
        out_shape=[
            jax.ShapeDtypeStruct((tokens, d_model), BF16),
            jax.ShapeDtypeStruct((nt, d_model, tt), BF16),
            jax.ShapeDtypeStruct((tokens, d_model), BF16),
        ],
        compiler_params=_params(("arbitrary",)),
        name=name,
    )(h2, pos_perm, inv, wqk, wv, gq, gk)


def _dil_attn_kernel(q_ref, ktp_ref, ktc_ref, vp_ref, vc_ref, o_ref, stat_ref, *, bounded):
    i = pl.program_id(1)
    blk = DIL_BLOCK
    has_prev = i > 0
    tq, d_model = q_ref.shape
    kt_all = jnp.concatenate([ktp_ref[...], ktc_ref[...]], axis=1)
    v_all = jnp.concatenate([vp_ref[...], vc_ref[...]], axis=0)
    r = lax.broadcasted_iota(jnp.int32, (blk, 2 * blk), 0)
    c = lax.broadcasted_iota(jnp.int32, (blk, 2 * blk), 1)
    band = (c >= r) & (c <= r + blk)
    first_mask = band & (has_prev | (c >= blk))
    lane = lax.broadcasted_iota(jnp.int32, (blk, 2 * HEAD_DIM), 1)
    low = lane < HEAD_DIM
    for u in range(tq // blk):
        rows = slice(u * blk, (u + 1) * blk)
        mask = first_mask if u == 0 else band
        stat_tile = jnp.zeros((blk, 2 * HEAD_DIM), F32)
        for pair in range(d_model // (2 * HEAD_DIM)):
            cols = slice(pair * 2 * HEAD_DIM, (pair + 1) * 2 * HEAD_DIM)
            qp = q_ref[rows, cols]
            kt = kt_all[cols, u * blk:(u + 2) * blk]
            vv = v_all[u * blk:(u + 2) * blk, cols]
            outs = []
            for hh, sel in ((0, low), (1, ~low)):
                qh = jnp.where(sel, qp, jnp.zeros_like(qp))
                s = jnp.dot(qh, kt, preferred_element_type=F32)
                s = jnp.where(mask, s, NEG_INF)
                if bounded:
                    p = jnp.exp2(s)
                    stat = jnp.sum(p, axis=1, keepdims=True)
                    outs.append(jnp.dot(p.astype(BF16), vv, preferred_element_type=F32))
                else:
                    m = jnp.max(s, axis=1, keepdims=True)
                    p = jnp.exp2(s - m)
                    den = jnp.sum(p, axis=1, keepdims=True)
                    stat = m + jnp.log2(den)
                    outs.append(jnp.dot(p.astype(BF16), vv, preferred_element_type=F32) * (1.0 / den))
                stat_tile = jnp.where(lane == 2 * pair + hh, stat, stat_tile)
            o_ref[rows, cols] = jnp.where(low, outs[0], outs[1]).astype(BF16)
        stat_ref[rows, :] = stat_tile


def _dil_attn(q, kt, v, *, n_seqs, tt, bounded, name):
    tokens, d_model = q.shape
    nq = tokens // tt // n_seqs
    per = tt // DIL_BLOCK
    tile = lambda s, i: s * nq + i
    prev_tile = lambda s, i: jnp.maximum(tile(s, i) - 1, 0)
    return pl.pallas_call(
        functools.partial(_dil_attn_kernel, bounded=bounded),
        grid=(n_seqs, nq),
        in_specs=[
            pl.BlockSpec((tt, d_model), lambda s, i: (tile(s, i), 0)),
            pl.BlockSpec((None, d_model, DIL_BLOCK), lambda s, i: (prev_tile(s, i), 0, per - 1)),
            pl.BlockSpec((None, d_model, tt), lambda s, i: (tile(s, i), 0, 0)),
            pl.BlockSpec((DIL_BLOCK, d_model), lambda s, i: ((prev_tile(s, i) + 1) * per - 1, 0)),
            pl.BlockSpec((tt, d_model), lambda s, i: (tile(s, i), 0)),
        ],
        out_specs=[
            pl.BlockSpec((tt, d_model), lambda s, i: (tile(s, i), 0)),
            pl.BlockSpec((tt, 2 * HEAD_DIM), lambda s, i: (tile(s, i), 0)),
        ],
        out_shape=[
            jax.ShapeDtypeStruct((tokens, d_model), BF16),
            jax.ShapeDtypeStruct((tokens, 2 * HEAD_DIM), F32),
        ],
        compiler_params=_params(("arbitrary", "arbitrary")),
        name=name,
    )(q, kt, kt, v, v)


def _dil_out_kernel(*refs, dils, bounded):
    n = len(dils)
    o_refs, stat_refs = refs[:n], refs[n:2 * n]
    e_ref, w_ref, x_ref, g_ref, w1_ref, w2_ref, y_ref, o_slab, stat_slab = refs[2 * n:]
    tt, d_model = x_ref.shape

    def natural(ref, dil, slab):
        if dil == 1:
            return ref[...].astype(F32)
        rows, width = tt // dil, ref.shape[-1]
        for r in range(dil):
            val = ref[r].astype(F32)
            for j in range(width // LANES):
                slab[j, pl.ds(r, rows, stride=dil), :] = val[:, j * LANES:(j + 1) * LANES]
        return jnp.concatenate([slab[j] for j in range(width // LANES)], axis=1)

    stats = [natural(ref, dil, stat_slab) for ref, dil in zip(stat_refs, dils)]
    outs = lambda g: natural(o_refs[g], dils[g], o_slab)

    def per_head_to_cols(w):
        hi = w.astype(BF16)
        lo = (w - hi.astype(F32)).astype(BF16)
        return jnp.dot(jnp.concatenate([hi, lo], axis=1), e_ref[...], preferred_element_type=F32)

    if bounded:
        lane = lax.broadcasted_iota(jnp.int32, stats[0].shape, 1)
        den = jnp.where(lane < d_model // HEAD_DIM, sum(stats[1:], stats[0]), 1.0)
        num = outs(0)
        for g in range(1, n):
            num = num + outs(g)
        merged = num * per_head_to_cols(1.0 / den)
    else:
        top = functools.reduce(jnp.maximum, stats)
        es = [jnp.exp2(l - top) for l in stats]
        den = sum(es[1:], es[0])
        merged = None
        for g in range(n):
            term = per_head_to_cols(es[g] / den) * outs(g)
            merged = term if merged is None else merged + term
    x1 = x_ref[...] + jnp.dot(merged.astype(BF16), w_ref[...], preferred_element_type=F32)
    y_ref[...] = _mlp_residual(x1, g_ref, w1_ref, w2_ref)


def _dil_out_mlp(outs, stats, expand, w, x2, g, w1, w2, *, batch, seq, tt, dils, bounded):
    tokens, d_model = x2.shape
    tpb = seq // tt

    def view_and_spec(a, dil):
        width = a.shape[-1]
        if dil == 1:
            return a, pl.BlockSpec((tt, width), lambda i: (i, 0))
        spec = pl.BlockSpec((None, dil, tt // dil, width), lambda i: (i // tpb, 0, i % tpb, 0))
        return a.reshape(batch, dil, seq // dil, width), spec

    args, specs = [], []
    for group in (outs, stats):
        for a, dil in zip(group, dils):
            a, spec = view_and_spec(a, dil)
            args.append(a)
            specs.append(spec)
    row = lambda i: (i, 0)
    return pl.pallas_call(
        functools.partial(_dil_out_kernel, dils=dils, bounded=bounded),
        grid=(tokens // tt,),
        in_specs=specs + [
            _resident((4 * HEAD_DIM, d_model)),
            _resident((d_model, d_model)),
            pl.BlockSpec((tt, d_model), row),
        ] + _mlp_specs(d_model, w1.shape[1]),
        out_specs=pl.BlockSpec((tt, d_model), row),
        out_shape=jax.ShapeDtypeStruct((tokens, d_model), F32),
        scratch_shapes=[pltpu.VMEM((d_model // LANES, tt, LANES), F32), pltpu.VMEM((1, tt, LANES), F32)],
        compiler_params=_params(("arbitrary",)),
        name="dil_outproj_mlp" + ("" if bounded else "_exact"),
    )(*args, expand, w, x2, g, w1, w2)


def _col(v):
    return v.reshape(-1, 1).astype(F32)


def _fox_layer(x2, norm_g, w_in, b_f, q_gain, k_gain, w_out, mlp, *, batch, seq, tt, tk, unroll):
    d_model = x2.shape[1]
    n_heads = d_model // HEAD_DIM
    tpb = seq // tt
    wt = w_in[:, :3 * d_model].T.astype(BF16)
    wf = w_in[:, 3 * d_model:].T.astype(BF16)
    qkv, c = _fox_inproj(x2, norm_g.reshape(1, -1), wt, wf, _col(b_f), _col(q_gain), _col(k_gain),
                         seq=seq, tt=tt)
    c4 = c.reshape(batch * tpb, n_heads, 1, tt)

    slack = 1.0 + 2.0 ** -7
    qb = math.sqrt(HEAD_DIM) * jnp.max(jnp.abs(q_gain)) * (LOG2E / math.sqrt(HEAD_DIM)) * slack
    kb = math.sqrt(HEAD_DIM) * jnp.max(jnp.abs(k_gain)) * slack
    per = tt // tk
    ch = c.reshape(batch, tpb, n_heads, tt).transpose(0, 2, 1, 3).reshape(batch, n_heads, seq)
    gap = ch[:, :, ::tt, None] - ch[:, :, None, tk - 1::tk]
    below = jnp.arange(seq // tk)[None, :] < (jnp.arange(tpb) * per)[:, None]
    jlo = jnp.sum((gap < -FOX_SKIP_LOG2) & below, axis=-1, dtype=jnp.int32).reshape(-1)

    ot = lax.cond(
        2.0 * qb * kb <= FOX_MAX_LOOSENESS_LOG2,
        lambda: _fox_attn_fast(qkv, c4, jlo, kb.reshape(1).astype(F32), batch=batch, seq=seq,
                               tk=tk, unroll=unroll),
        lambda: _fox_attn_exact(qkv, c4, batch=batch, seq=seq),
    )
    return _fox_out_mlp(ot, w_out.astype(BF16), x2, *mlp)


def _dil_layer(x2, positions, norm_g, w_in, q_gain, k_gain, w_out, mlp, *, batch, seq, tt):
    d_model = x2.shape[1]
    n_heads = d_model // HEAD_DIM
    dils = tuple(dil for _, dil in DIL_GROUPS)
    assert all(window // dil == DIL_BLOCK for window, dil in DIL_GROUPS)
    inv = (ROPE_THETA ** (-np.arange(ROT_DIM // 2, dtype=np.float64) * 2.0 / ROT_DIM)).astype(np.float32)
    inv = jnp.asarray(inv).reshape(-1, 1)
    hs = _dil_prep(x2, norm_g.reshape(1, -1), batch=batch, seq=seq, tt=tt, dils=dils)
    qkvs = []
    for g, dil in enumerate(dils):
        base = g * 3 * d_model
        wqk = w_in[:, base:base + 2 * d_model].T.astype(BF16)
        wv = w_in[:, base + 2 * d_model:base + 3 * d_model].astype(BF16)
        pos_perm = positions.reshape(batch, seq // dil, dil).transpose(0, 2, 1).reshape(1, -1)
        qkvs.append(_dil_inproj(hs[g].reshape(batch * seq, d_model), pos_perm, inv, wqk, wv,
                                _col(q_gain[g]), _col(k_gain[g]), tt=tt, name=f"dil_inproj_d{dil}"))
    head_of_col = np.arange(d_model) // HEAD_DIM
    expand = (np.arange(2 * HEAD_DIM)[:, None] == head_of_col[None, :]).astype(np.float32)
    expand = jnp.asarray(np.concatenate([expand, expand], axis=0), BF16)
    assert n_heads <= 2 * HEAD_DIM
    w_out = w_out.astype(BF16)

    def attend_and_merge(bounded):
        outs, stats = [], []
        for (q, kt, v), dil in zip(qkvs, dils):
            o, stat = _dil_attn(q, kt, v, n_seqs=batch * dil, tt=tt, bounded=bounded,
                                name=f"dil_attn_d{dil}" + ("" if bounded else "_exact"))
            outs.append(o)
            stats.append(stat)
        return _dil_out_mlp(outs, stats, expand, w_out, x2, *mlp, batch=batch, seq=seq, tt=tt, dils=dils,
                            bounded=bounded)

    bound = (math.sqrt(HEAD_DIM) * LOG2E * (1.0 + 2.0 ** -6)
             * jnp.max(jnp.max(jnp.abs(q_gain), axis=1) * jnp.max(jnp.abs(k_gain), axis=1)))
    return lax.cond(bound <= DIL_MAX_SCORE_LOG2, functools.partial(attend_and_merge, True),
                    functools.partial(attend_and_merge, False))


def _forward(x, positions, fox_norm, fox_w_in, fox_b_f, fox_q_gain, fox_k_gain, fox_w_out,
             dil_norm, dil_w_in, dil_q_gain, dil_k_gain, dil_w_out, mlp_norm, mlp_w1, mlp_w2,
             *, tt, tk_fox, unroll_fox):
    batch, seq, d_model = x.shape
    assert seq % (tt * max(dil for _, dil in DIL_GROUPS)) == 0 and tt % tk_fox == 0
    x2 = x.reshape(batch * seq, d_model)
    depth = mlp_norm.shape[0]
    for layer in range(depth):
        j = layer // 2
        mlp = (mlp_norm[layer].reshape(1, -1), mlp_w1[layer].astype(BF16), mlp_w2[layer].astype(BF16))
        if layer % 2 == 0:
            x2 = _fox_layer(x2, fox_norm[j], fox_w_in[j], fox_b_f[j], fox_q_gain[j], fox_k_gain[j],
                            fox_w_out[j], mlp, batch=batch, seq=seq, tt=tt, tk=tk_fox, unroll=unroll_fox)
        else:
            x2 = _dil_layer(x2, positions, dil_norm[j], dil_w_in[j], dil_q_gain[j], dil_k_gain[j],
                            dil_w_out[j], mlp, batch=batch, seq=seq, tt=tt)
    return x2.reshape(batch, seq, d_model)


def kernel(x, positions, fox_norm, fox_w_in, fox_b_f, fox_q_gain, fox_k_gain, fox_w_out, dil_norm, dil_w_in, dil_q_gain, dil_k_gain, dil_w_out, mlp_norm, mlp_w1, mlp_w2):
    return _forward(x, positions, fox_norm, fox_w_in, fox_b_f, fox_q_gain, fox_k_gain, fox_w_out,
                    dil_norm, dil_w_in, dil_q_gain, dil_k_gain, dil_w_out, mlp_norm, mlp_w1, mlp_w2,
                    tt=512, tk_fox=256, unroll_fox=(24, 12, 6, 2))
```
